```python
import jax, jax.numpy as jnp
from jax import lax
import numpy as np

D_MODEL = 1024
BATCH = 8
SEQ = 4096
DEPTH = 2

GRID_W = 64
N_META = 16
HEAD_DIM = 64
ATTN_HEADS = 8
ATTN_KV_HEADS = 2
ATTN_GROUP = ATTN_HEADS // ATTN_KV_HEADS
ATTN_WIDTH = ATTN_HEADS * HEAD_DIM
ATTN_KV_WIDTH = ATTN_KV_HEADS * HEAD_DIM
NA_HEADS = 8
NA_WIDTH = NA_HEADS * HEAD_DIM
NA_KH_MAX = 8
NA_KW = 16
Q_BLOCK = 128
ROPE_THETA = 10000.0
EPS = 1e-6
IN_SPLITS = (ATTN_WIDTH, ATTN_KV_WIDTH, ATTN_KV_WIDTH, ATTN_WIDTH,
             NA_WIDTH, NA_WIDTH, NA_WIDTH, NA_WIDTH, D_MODEL, D_MODEL)
IN_COLS = ATTN_WIDTH * 2 + ATTN_KV_WIDTH * 2 + NA_WIDTH * 4 + D_MODEL * 2

kernel_name = "hybrid_gqa_natten_gated_encoder"


def rms_norm(x, g):
    xf = x.astype(jnp.float32)
    y = xf * lax.rsqrt(jnp.mean(xf * xf, axis=-1, keepdims=True) + EPS)
    return (y * g.astype(jnp.float32)).astype(x.dtype)


def split_cols(p):
    outs, off = [], 0
    for w in IN_SPLITS:
        outs.append(p[..., off:off + w])
        off += w
    return outs


def axial_rope_tables(n_real):
    t = jnp.arange(n_real, dtype=jnp.int32)
    zeros = jnp.zeros((N_META,), jnp.int32)
    row = jnp.concatenate([zeros, t // GRID_W]).astype(jnp.float32)
    col = jnp.concatenate([zeros, t % GRID_W]).astype(jnp.float32)
    axis_dim = HEAD_DIM // 2
    inv = ROPE_THETA ** (-jnp.arange(0, axis_dim, 2, dtype=jnp.float32) / axis_dim)
    ang_r = row[:, None] * inv[None]
    ang_c = col[:, None] * inv[None]
    ang = jnp.concatenate([ang_r, ang_r, ang_c, ang_c], axis=-1)
    return jnp.cos(ang), jnp.sin(ang)


def apply_axial_rope(x, cos, sin):
    axis_dim = HEAD_DIM // 2
    q4 = axis_dim // 2
    xf = x.astype(jnp.float32)
    xr, xc = xf[..., :axis_dim], xf[..., axis_dim:]
    rot = jnp.concatenate([-xr[..., q4:], xr[..., :q4], -xc[..., q4:], xc[..., :q4]], axis=-1)
    return (xf * cos[None, :, None, :] + rot * sin[None, :, None, :]).astype(x.dtype)


def gqa_attention(q, k, v):
    b, l = q.shape[:2]
    n = l - N_META
    scale = HEAD_DIM ** -0.5

    def attend(qb):
        s = jnp.einsum('bqkgd,bskd->bkgqs', qb, k).astype(jnp.float32) * scale
        p = jax.nn.softmax(s, axis=-1).astype(v.dtype)
        return jnp.einsum('bkgqs,bskd->bqkgd', p, v)

    qg = q.reshape(b, l, ATTN_KV_HEADS, ATTN_GROUP, HEAD_DIM)
    o_meta = attend(qg[:, :N_META])
    q_blocks = qg[:, N_META:].reshape(b, n // Q_BLOCK, Q_BLOCK, ATTN_KV_HEADS, ATTN_GROUP, HEAD_DIM)
    q_blocks = q_blocks.transpose(1, 0, 2, 3, 4, 5)
    o_real = lax.map(attend, q_blocks).transpose(1, 0, 2, 3, 4, 5)
    o_real = o_real.reshape(b, n, ATTN_KV_HEADS, ATTN_GROUP, HEAD_DIM)
    return jnp.concatenate([o_meta, o_real], axis=1).reshape(b, l, ATTN_WIDTH)


def neighbourhood_attention(q, k, v, rpb):
    b, l = q.shape[:2]
    n = l - N_META
    rows = n // GRID_W
    kh = min(NA_KH_MAX, rows)
    scale = HEAD_DIM ** -0.5
    qm, km, vm = q[:, :N_META], k[:, :N_META], v[:, :N_META]

    s_mm = jnp.einsum('bqhd,bmhd->bhqm', qm, km).astype(jnp.float32) * scale
    o_meta = jnp.einsum('bhqm,bmhd->bqhd', jax.nn.softmax(s_mm, axis=-1).astype(v.dtype), vm)

    qg = q[:, N_META:].reshape(b, rows, GRID_W, NA_HEADS, HEAD_DIM)
    kg = k[:, N_META:].reshape(b, rows, GRID_W, NA_HEADS, HEAD_DIM)
    vg = v[:, N_META:].reshape(b, rows, GRID_W, NA_HEADS, HEAD_DIM)

    r = jnp.arange(rows)
    row_start = jnp.clip(r - kh // 2, 0, rows - kh)
    row_bias_idx = row_start[:, None] + jnp.arange(kh)[None] - r[:, None] + NA_KH_MAX - 1
    c = jnp.arange(GRID_W)
    col_start = jnp.clip(c - NA_KW // 2, 0, GRID_W - NA_KW)
    col_idx = col_start[:, None] + jnp.arange(NA_KW)[None]
    col_bias_idx = col_idx - c[:, None] + NA_KW - 1
    rpb_cols = rpb[:, :, col_bias_idx]

    def row_step(args):
        q_row, rs, rbi = args
        k_band = lax.dynamic_slice_in_dim(kg, rs, kh, axis=1)
        v_band = lax.dynamic_slice_in_dim(vg, rs, kh, axis=1)
        k_win = k_band[:, :, col_idx]
        v_win = v_band[:, :, col_idx]
        s_win = jnp.einsum('bchd,bicjhd->bhcij', q_row, k_win).astype(jnp.float32) * scale
        bias = rpb_cols[:, rbi].transpose(0, 2, 1, 3)
        s_win = s_win + bias[None].astype(jnp.float32)
        s_meta = jnp.einsum('bchd,bmhd->bhcm', q_row, km).astype(jnp.float32) * scale
        s = jnp.concatenate([s_win.reshape(b, NA_HEADS, GRID_W, kh * NA_KW), s_meta], axis=-1)
        p = jax.nn.softmax(s, axis=-1).astype(v.dtype)
        p_win = p[..., :kh * NA_KW].reshape(b, NA_HEADS, GRID_W, kh, NA_KW)
        p_meta = p[..., kh * NA_KW:]
        return (jnp.einsum('bhcij,bicjhd->bchd', p_win, v_win)
                + jnp.einsum('bhcm,bmhd->bchd', p_meta, vm))

    o_rows = lax.map(row_step, (qg.transpose(1, 0, 2, 3, 4), row_start, row_bias_idx))
    o_real = o_rows.transpose(1, 0, 2, 3, 4).reshape(b, n, NA_WIDTH)
    return jnp.concatenate([o_meta.reshape(b, N_META, NA_WIDTH), o_real], axis=1)


def hybrid_layer(x, norm_g, w_in, q_norm_g, k_norm_g, rpb, w_o_attn, w_o_na, w_out, cos, sin):
    b, l, _ = x.shape
    h = rms_norm(x, norm_g)
    proj = jnp.einsum('bld,de->ble', h, w_in)
    qa, ka, va, za, qb, kb, vb, zb, ga, gb = split_cols(proj)

    qa = apply_axial_rope(rms_norm(qa.reshape(b, l, ATTN_HEADS, HEAD_DIM), q_norm_g), cos, sin)
    ka = apply_axial_rope(rms_norm(ka.reshape(b, l, ATTN_KV_HEADS, HEAD_DIM), k_norm_g), cos, sin)
    va = va.reshape(b, l, ATTN_KV_HEADS, HEAD_DIM)
    oa = gqa_attention(qa, ka, va) * jax.nn.silu(za)

    qb = qb.reshape(b, l, NA_HEADS, HEAD_DIM)
    kb = kb.reshape(b, l, NA_HEADS, HEAD_DIM)
    vb = vb.reshape(b, l, NA_HEADS, HEAD_DIM)
    ob = neighbourhood_attention(qb, kb, vb, rpb) * jax.nn.silu(zb)

    ya = jnp.einsum('ble,ed->bld', oa, w_o_attn)
    yb = jnp.einsum('ble,ed->bld', ob, w_o_na)
    mixed = jax.nn.sigmoid(ga) * ya + jax.nn.sigmoid(gb) * yb
    return x + jnp.einsum('bld,de->ble', mixed, w_out)


def setup_inputs(seed: int = 0) -> dict:
    key = jax.random.key(seed)
    ks = jax.random.split(key, 12)
    f32 = jnp.float32
    x = jax.random.normal(ks[0], (BATCH, SEQ, D_MODEL), f32)
    meta_tokens = jax.random.normal(ks[1], (N_META, D_MODEL), f32)
    norm_g = 1.0 + 0.05 * jax.random.normal(ks[2], (DEPTH, D_MODEL), f32)
    w_in = jax.random.normal(ks[3], (DEPTH, D_MODEL, IN_COLS), f32) * D_MODEL ** -0.5
    q_norm_g = 1.0 + 0.05 * jax.random.normal(ks[4], (DEPTH, HEAD_DIM), f32)
    k_norm_g = 1.0 + 0.05 * jax.random.normal(ks[5], (DEPTH, HEAD_DIM), f32)
    na_rpb = 0.1 * jax.random.normal(ks[6], (DEPTH, NA_HEADS, 2 * NA_KH_MAX - 1, 2 * NA_KW - 1), f32)
    w_o_attn = jax.random.normal(ks[7], (DEPTH, ATTN_WIDTH, D_MODEL), f32) * ATTN_WIDTH ** -0.5
    w_o_na = jax.random.normal(ks[8], (DEPTH, NA_WIDTH, D_MODEL), f32) * NA_WIDTH ** -0.5
    w_out = jax.random.normal(ks[9], (DEPTH, D_MODEL, D_MODEL), f32) * D_MODEL ** -0.5
    final_norm_g = 1.0 + 0.05 * jax.random.normal(ks[10], (D_MODEL,), f32)
    return {"x": x, "meta_tokens": meta_tokens, "norm_g": norm_g, "w_in": w_in,
            "q_norm_g": q_norm_g, "k_norm_g": k_norm_g, "na_rpb": na_rpb,
            "w_o_attn": w_o_attn, "w_o_na": w_o_na, "w_out": w_out,
            "final_norm_g": final_norm_g}


def reference(x, meta_tokens, norm_g, w_in, q_norm_g, k_norm_g, na_rpb, w_o_attn, w_o_na, w_out, final_norm_g):
    b, s, _ = x.shape
    meta = jnp.broadcast_to(meta_tokens[None].astype(x.dtype), (b, N_META, D_MODEL))
    h = jnp.concatenate([meta, x], axis=1)
    cos, sin = axial_rope_tables(s)
    for i in range(DEPTH):
        h = hybrid_layer(h, norm_g[i], w_in[i], q_norm_g[i], k_norm_g[i], na_rpb[i],
                         w_o_attn[i], w_o_na[i], w_out[i], cos, sin)
    h = rms_norm(h, final_norm_g)
    return h[:, N_META:]
```

```python
import functools

import numpy as np
import jax
import jax.numpy as jnp
from jax import lax
from jax.experimental import pallas as pl
from jax.experimental.pallas import tpu as pltpu

D_MODEL = 1024
SEQ = 4096
GRID_W = 64
GRID_ROWS = SEQ // GRID_W
N_META = 16
META_PAD = 128
HEAD_DIM = 64
ATTN_HEADS = 8
ATTN_KV_HEADS = 2
ATTN_GROUP = ATTN_HEADS // ATTN_KV_HEADS
NA_HEADS = 8
NA_KH = 8
NA_KW = 16
ROPE_THETA = 10000.0
EPS = 1e-6
SCALE = HEAD_DIM ** -0.5
NEG = -1e30

ATTN_WIDTH = ATTN_HEADS * HEAD_DIM
KV_WIDTH = ATTN_KV_HEADS * HEAD_DIM
NA_WIDTH = NA_HEADS * HEAD_DIM

P1_QA, P1_KA, P1_VA, P1_QB, P1_KB, P1_VB, P1_END = 0, 512, 640, 768, 1280, 1792, 2304
P2_ZA, P2_ZB, P2_GA, P2_GB, P2_END = 0, 512, 1024, 2048, 3072

NA_QROWS = 4
NA_QBLK = NA_QROWS * GRID_W
NA_BAND_ROWS = 12
NA_BAND = NA_BAND_ROWS * GRID_W
NA_NBLK = GRID_ROWS // NA_QROWS

TOKEN_TILE = 512
GQA_TQ = 512
GQA_TK = 512
VMEM_LIMIT = 56 * 1024 * 1024

bf16 = jnp.bfloat16
f32 = jnp.float32


def _nn(a, b):
    return lax.dot_general(a, b, (((1,), (0,)), ((), ())), preferred_element_type=f32)


def _nt(a, b):
    return lax.dot_general(a, b, (((1,), (1,)), ((), ())), preferred_element_type=f32)


def _tn(a, b):
    return lax.dot_general(a, b, (((0,), (0,)), ((), ())), preferred_element_type=f32)


def _rms_rows(x, g):
    return x * lax.rsqrt(jnp.mean(x * x, axis=-1, keepdims=True) + EPS) * g


def _sigmoid(x):
    return 1.0 / (1.0 + jnp.exp(-x))


def _params(sem):
    return pltpu.CompilerParams(dimension_semantics=sem, vmem_limit_bytes=VMEM_LIMIT)


def _proj_kernel(x_ref, g_ref, w_ref, qg_ref, kg_ref, cos_ref, sin_ref,
                 qta_ref, ka_ref, vta_ref, qtb_ref, kb_ref, vtb_ref):
    h = _rms_rows(x_ref[0], g_ref[...]).astype(bf16)
    cos = cos_ref[...]
    sin = sin_ref[...]

    def proj(lo, hi):
        return _nt(w_ref[lo:hi, :], h)

    def norm_rope(t, gain):
        y = t * lax.rsqrt(jnp.mean(t * t, axis=0, keepdims=True) + EPS) * gain
        sw = jnp.concatenate([y[16:32], y[0:16], y[48:64], y[32:48]], axis=0)
        return y * cos + sw * sin

    qg = qg_ref[...]
    kg = kg_ref[...]
    qa = proj(P1_QA, P1_KA)
    for hd in range(ATTN_HEADS):
        lo = hd * HEAD_DIM
        qta_ref[0, lo:lo + HEAD_DIM, :] = (norm_rope(qa[lo:lo + HEAD_DIM], qg) * SCALE).astype(bf16)
    ka = proj(P1_KA, P1_VA)
    kt = jnp.concatenate([norm_rope(ka[0:HEAD_DIM], kg), norm_rope(ka[HEAD_DIM:], kg)], axis=0)
    ka_ref[0] = kt.T.astype(bf16)
    vta_ref[0] = proj(P1_VA, P1_QB).astype(bf16)
    qtb_ref[0] = (proj(P1_QB, P1_KB) * SCALE).astype(bf16)
    kbt = proj(P1_KB, P1_VB)
    for p in range(NA_HEADS // 2):
        kb_ref[0, p] = kbt[128 * p:128 * (p + 1)].T.astype(bf16)
    vtb_ref[0] = proj(P1_VB, P1_END).astype(bf16)


def _proj_call(x3, norm_g, w1t, qg, kg, cos_t, sin_t, tm):
    bx, lx, _ = x3.shape
    grid = (bx, lx // tm)
    out_shape = (
        jax.ShapeDtypeStruct((bx, ATTN_WIDTH, lx), bf16),
        jax.ShapeDtypeStruct((bx, lx, KV_WIDTH), bf16),
        jax.ShapeDtypeStruct((bx, KV_WIDTH, lx), bf16),
        jax.ShapeDtypeStruct((bx, NA_WIDTH, lx), bf16),
        jax.ShapeDtypeStruct((bx, NA_HEADS // 2, lx, 128), bf16),
        jax.ShapeDtypeStruct((bx, NA_WIDTH, lx), bf16),
    )
    return pl.pallas_call(
        _proj_kernel,
        grid=grid,
        in_specs=[
            pl.BlockSpec((1, tm, D_MODEL), lambda b, j: (b, j, 0)),
            pl.BlockSpec((1, D_MODEL), lambda b, j: (0, 0)),
            pl.BlockSpec((P1_END, D_MODEL), lambda b, j: (0, 0)),
            pl.BlockSpec((HEAD_DIM, 1), lambda b, j: (0, 0)),
            pl.BlockSpec((HEAD_DIM, 1), lambda b, j: (0, 0)),
            pl.BlockSpec((HEAD_DIM, tm), lambda b, j: (0, j)),
            pl.BlockSpec((HEAD_DIM, tm), lambda b, j: (0, j)),
        ],
        out_specs=(
            pl.BlockSpec((1, ATTN_WIDTH, tm), lambda b, j: (b, 0, j)),
            pl.BlockSpec((1, tm, KV_WIDTH), lambda b, j: (b, j, 0)),
            pl.BlockSpec((1, KV_WIDTH, tm), lambda b, j: (b, 0, j)),
            pl.BlockSpec((1, NA_WIDTH, tm), lambda b, j: (b, 0, j)),
            pl.BlockSpec((1, NA_HEADS // 2, tm, 128), lambda b, j: (b, 0, j, 0)),
            pl.BlockSpec((1, NA_WIDTH, tm), lambda b, j: (b, 0, j)),
        ),
        out_shape=out_shape,
        compiler_params=_params(("parallel", "parallel")),
        name="qkv_proj",
    )(x3, norm_g, w1t, qg, kg, cos_t, sin_t)


def _gqa_kernel(q_ref, k_ref, km_ref, vt_ref, vtm_ref, o_ref, *, tq, tk):
    g = pl.program_id(1)
    row = lax.broadcasted_iota(jnp.int32, (2 * HEAD_DIM, tq), 0)
    sel = (row >= HEAD_DIM) == (g == 1)
    meta_valid = row < N_META
    n_chunks = k_ref.shape[1] // tk

    def head(hh, carry):
        off = pl.multiple_of(hh * HEAD_DIM, HEAD_DIM)
        q = q_ref[0, pl.ds(off, HEAD_DIM), :]
        qext = jnp.where(sel, jnp.concatenate([q, q], axis=0), jnp.zeros((), bf16))
        s = jnp.where(meta_valid, _nn(km_ref[0], qext), NEG)
        m = jnp.max(s, axis=0, keepdims=True)
        p = jnp.exp(s - m)
        l = jnp.sum(p, axis=0, keepdims=True)
        acc = _nn(vtm_ref[0], p.astype(bf16))
        for c in range(n_chunks):
            s = _nn(k_ref[0, c * tk:(c + 1) * tk, :], qext)
            m_new = jnp.maximum(m, jnp.max(s, axis=0, keepdims=True))
            alpha = jnp.exp(m - m_new)
            p = jnp.exp(s - m_new)
            l = alpha * l + jnp.sum(p, axis=0, keepdims=True)
            acc = alpha * acc + _nn(vt_ref[0, :, c * tk:(c + 1) * tk], p.astype(bf16))
            m = m_new
        o_ref[0, pl.ds(off, HEAD_DIM), :] = (acc / l).astype(bf16)
        return carry

    lax.fori_loop(0, ATTN_GROUP, head, 0)


def _gqa_call(qt, k_real, k_meta, vt_real, vt_meta, *, meta_queries):
    b = k_real.shape[0]
    gw = ATTN_GROUP * HEAD_DIM
    if meta_queries:
        tq, nq = META_PAD, 1
        q_map = lambda bi, g, j: (0, g, bi)
    else:
        tq, nq = GQA_TQ, SEQ // GQA_TQ
        q_map = lambda bi, g, j: (bi, g, j)
    return pl.pallas_call(
        functools.partial(_gqa_kernel, tq=tq, tk=GQA_TK),
        grid=(b, ATTN_KV_HEADS, nq),
        in_specs=[
            pl.BlockSpec((1, gw, tq), q_map),
            pl.BlockSpec((1, SEQ, KV_WIDTH), lambda bi, g, j: (bi, 0, 0)),
            pl.BlockSpec((1, META_PAD, KV_WIDTH), lambda bi, g, j: (0, bi, 0)),
            pl.BlockSpec((1, HEAD_DIM, SEQ), lambda bi, g, j: (bi, g, 0)),
            pl.BlockSpec((1, HEAD_DIM, META_PAD), lambda bi, g, j: (0, g, bi)),
        ],
        out_specs=pl.BlockSpec((1, gw, tq), q_map),
        out_shape=jax.ShapeDtypeStruct(qt.shape, bf16),
        compiler_params=_params(("parallel", "parallel", "parallel")),
        name="gqa_meta" if meta_queries else "gqa_attn",
    )(qt, k_real, k_meta, vt_real, vt_meta)


def _na_block_geometry(blk):
    r0 = blk * NA_QROWS
    kb = min(max(r0 - NA_KH // 2, 0), GRID_ROWS - NA_BAND_ROWS)
    typ = 0 if blk == 0 else (2 if blk == NA_NBLK - 1 else 1)
    return r0, kb, typ


def _na_kernel(q_ref, k_ref, km_ref, vt_ref, vtm_ref, tab_ref, o_ref):
    par = pl.program_id(0) % 2
    row = lax.broadcasted_iota(jnp.int32, (2 * HEAD_DIM, NA_QBLK), 0)
    sel = (row >= HEAD_DIM) == (par == 1)
    meta_valid = row < N_META
    km = km_ref[0, 0]
    vtm = vtm_ref[0]
    for blk in range(NA_NBLK):
        _, kb, typ = _na_block_geometry(blk)
        ks = kb * GRID_W
        qs = blk * NA_QBLK
        q = q_ref[0, :, qs:qs + NA_QBLK]
        qext = jnp.where(sel, jnp.concatenate([q, q], axis=0), jnp.zeros((), bf16))
        s = _nn(k_ref[0, 0, ks:ks + NA_BAND, :], qext) + tab_ref[typ, 0]
        sm = jnp.where(meta_valid, _nn(km, qext), NEG)
        m = jnp.maximum(jnp.max(s, axis=0, keepdims=True), jnp.max(sm, axis=0, keepdims=True))
        p = jnp.exp(s - m)
        pm = jnp.exp(sm - m)
        l = jnp.sum(p, axis=0, keepdims=True) + jnp.sum(pm, axis=0, keepdims=True)
        acc = _nn(vt_ref[0, :, ks:ks + NA_BAND], p.astype(bf16)) + _nn(vtm, pm.astype(bf16))
        o_ref[0, :, qs:qs + NA_QBLK] = (acc / l).astype(bf16)


def _na_call(qt, k_real, k_meta, vt_real, vt_meta, table):
    b = qt.shape[0]
    return pl.pallas_call(
        _na_kernel,
        grid=(NA_HEADS, b),
        in_specs=[
            pl.BlockSpec((1, HEAD_DIM, SEQ), lambda h, bi: (bi, h, 0)),
            pl.BlockSpec((1, 1, SEQ, 128), lambda h, bi: (bi, h // 2, 0, 0)),
            pl.BlockSpec((1, 1, META_PAD, 128), lambda h, bi: (0, h // 2, bi, 0)),
            pl.BlockSpec((1, HEAD_DIM, SEQ), lambda h, bi: (bi, h, 0)),
            pl.BlockSpec((1, HEAD_DIM, META_PAD), lambda h, bi: (0, h, bi)),
            pl.BlockSpec((3, 1, NA_BAND, NA_QBLK), lambda h, bi: (0, h, 0, 0)),
        ],
        out_specs=pl.BlockSpec((1, HEAD_DIM, SEQ), lambda h, bi: (bi, h, 0)),
        out_shape=jax.ShapeDtypeStruct(qt.shape, bf16),
        compiler_params=_params(("parallel", "parallel")),
        name="na_attn",
    )(qt, k_real, k_meta, vt_real, vt_meta, table)


def _na_meta_kernel(q_ref, km_ref, vtm_ref, o_ref):
    row = lax.broadcasted_iota(jnp.int32, (2 * HEAD_DIM, META_PAD), 0)
    meta_valid = row < N_META
    for h in range(NA_HEADS):
        lo = h * HEAD_DIM
        q = q_ref[0, lo:lo + HEAD_DIM, :]
        sel = (row >= HEAD_DIM) == (h % 2 == 1)
        qext = jnp.where(sel, jnp.concatenate([q, q], axis=0), jnp.zeros((), bf16))
        s = jnp.where(meta_valid, _nn(km_ref[0, h // 2], qext), NEG)
        m = jnp.max(s, axis=0, keepdims=True)
        p = jnp.exp(s - m)
        l = jnp.sum(p, axis=0, keepdims=True)
        acc = _nn(vtm_ref[0, lo:lo + HEAD_DIM, :], p.astype(bf16))
        o_ref[0, lo:lo + HEAD_DIM, :] = (acc / l).astype(bf16)


def _na_meta_call(qt_meta, k_meta, vt_meta, b):
    return pl.pallas_call(
        _na_meta_kernel,
        grid=(b,),
        in_specs=[
            pl.BlockSpec((1, NA_WIDTH, META_PAD), lambda bi: (0, 0, bi)),
            pl.BlockSpec((1, NA_HEADS // 2, META_PAD, 128), lambda bi: (0, 0, bi, 0)),
            pl.BlockSpec((1, NA_WIDTH, META_PAD), lambda bi: (0, 0, bi)),
        ],
        out_specs=pl.BlockSpec((1, NA_WIDTH, META_PAD), lambda bi: (0, 0, bi)),
        out_shape=jax.ShapeDtypeStruct(qt_meta.shape, bf16),
        compiler_params=_params(("parallel",)),
        name="na_meta",
    )(qt_meta, k_meta, vt_meta)


def _out_kernel(x_ref, g_ref, ata_ref, atb_ref, w2_ref, woa_ref, wob_ref, wout_ref, fg_ref,
                o_ref, *, final):
    x = x_ref[0]
    h = _rms_rows(x, g_ref[...]).astype(bf16)

    def gated(at_ref, lo, hi):
        z = _nt(w2_ref[lo:hi, :], h)
        return (at_ref[0].astype(f32) * (z * _sigmoid(z))).astype(bf16)

    ya = _nn(woa_ref[...], gated(ata_ref, P2_ZA, P2_ZB))
    yb = _nn(wob_ref[...], gated(atb_ref, P2_ZB, P2_GA))
    ga = _nt(w2_ref[P2_GA:P2_GB, :], h)
    gb = _nt(w2_ref[P2_GB:P2_END, :], h)
    mix = (_sigmoid(ga) * ya + _sigmoid(gb) * yb).astype(bf16)
    out = x + _tn(mix, wout_ref[...])
    if final:
        out = _rms_rows(out, fg_ref[...])
    o_ref[0] = out


def _out_call(x3, norm_g, at_a, at_b, w2t, woat, wobt, wout, final_g, tm, final):
    bx, lx, _ = x3.shape
    const = lambda b, j: (0, 0)
    return pl.pallas_call(
        functools.partial(_out_kernel, final=final),
        grid=(bx, lx // tm),
        in_specs=[
            pl.BlockSpec((1, tm, D_MODEL), lambda b, j: (b, j, 0)),
            pl.BlockSpec((1, D_MODEL), const),
            pl.BlockSpec((1, ATTN_WIDTH, tm), lambda b, j: (b, 0, j)),
            pl.BlockSpec((1, NA_WIDTH, tm), lambda b, j: (b, 0, j)),
            pl.BlockSpec((P2_END, D_MODEL), const),
            pl.BlockSpec((D_MODEL, ATTN_WIDTH), const),
            pl.BlockSpec((D_MODEL, NA_WIDTH), const),
            pl.BlockSpec((D_MODEL, D_MODEL), const),
            pl.BlockSpec((1, D_MODEL), const),
        ],
        out_specs=pl.BlockSpec((1, tm, D_MODEL), lambda b, j: (b, j, 0)),
        out_shape=jax.ShapeDtypeStruct(x3.shape, f32),
        compiler_params=_params(("parallel", "parallel")),
        name="gate_out_proj",
    )(x3, norm_g, at_a, at_b, w2t, woat, wobt, wout, final_g)


def _rope_tables():
    t = np.arange(SEQ)
    axis_dim = HEAD_DIM // 2
    inv = jnp.asarray(ROPE_THETA, f32) ** (-jnp.arange(0, axis_dim, 2, dtype=f32) / axis_dim)
    ang_r = jnp.asarray(t // GRID_W, f32)[None, :] * inv[:, None]
    ang_c = jnp.asarray(t % GRID_W, f32)[None, :] * inv[:, None]
    ang = jnp.concatenate([ang_r, ang_r, ang_c, ang_c], axis=0)
    sign = np.repeat(np.array([-1.0, 1.0, -1.0, 1.0], np.float32), axis_dim // 2)[:, None]
    cos_r, sin_r = jnp.cos(ang), jnp.sin(ang) * sign
    n_meta_cols = META_PAD
    return cos_r, sin_r, n_meta_cols


def _na_tables(rpb):
    tabs = []
    i = np.arange(NA_BAND_ROWS)
    j = np.arange(NA_QROWS)
    kc = np.arange(GRID_W)
    c = np.arange(GRID_W)
    cs = np.clip(c - NA_KW // 2, 0, GRID_W - NA_KW)
    cvalid = (kc[:, None] >= cs[None]) & (kc[:, None] < cs[None] + NA_KW)
    cidx = np.clip(kc[:, None] - c[None] + NA_KW - 1, 0, 2 * NA_KW - 2)
    for blk in (0, 1, NA_NBLK - 1):
        r0, kb, _ = _na_block_geometry(blk)
        kr = kb + i
        r = r0 + j
        rs = np.clip(r - NA_KH // 2, 0, GRID_ROWS - NA_KH)
        rvalid = (kr[:, None] >= rs[None]) & (kr[:, None] < rs[None] + NA_KH)
        ridx = np.clip(kr[:, None] - r[None] + NA_KH - 1, 0, 2 * NA_KH - 2)
        shape4 = (NA_BAND_ROWS, GRID_W, NA_QROWS, GRID_W)
        valid = np.broadcast_to(rvalid[:, None, :, None] & cvalid[None, :, None, :], shape4)
        ridx4 = np.broadcast_to(ridx[:, None, :, None], shape4).reshape(NA_BAND, NA_QBLK)
        cidx4 = np.broadcast_to(cidx[None, :, None, :], shape4).reshape(NA_BAND, NA_QBLK)
        vals = rpb[:, ridx4, cidx4]
        tabs.append(jnp.where(valid.reshape(NA_BAND, NA_QBLK)[None], vals, NEG))
    return jnp.stack(tabs, axis=0).astype(f32)


def kernel(x, meta_tokens, norm_g, w_in, q_norm_g, k_norm_g, na_rpb, w_o_attn, w_o_na, w_out,
           final_norm_g):
    b = x.shape[0]
    depth = w_in.shape[0]
    cos_r, sin_r, _ = _rope_tables()
    cos_m = jnp.ones((HEAD_DIM, b * META_PAD), f32)
    sin_m = jnp.zeros((HEAD_DIM, b * META_PAD), f32)

    meta_pad = jnp.concatenate(
        [meta_tokens.astype(f32), jnp.zeros((META_PAD - N_META, D_MODEL), f32)], axis=0)
    xm = jnp.tile(meta_pad, (b, 1))[None]
    xr = x
    fg = final_norm_g.reshape(1, D_MODEL)

    for i in range(depth):
        wi = w_in[i]
        qa, ka, va, za, qb, kb, vb, zb, ga, gb = (
            wi[:, 0:512], wi[:, 512:640], wi[:, 640:768], wi[:, 768:1280], wi[:, 1280:1792],
            wi[:, 1792:2304], wi[:, 2304:2816], wi[:, 2816:3328], wi[:, 3328:4352], wi[:, 4352:5376])
        w1t = jnp.concatenate([qa, ka, va, qb, kb, vb], axis=1).T.astype(bf16)
        w2t = jnp.concatenate([za, zb, ga, gb], axis=1).T.astype(bf16)
        woat = w_o_attn[i].T.astype(bf16)
        wobt = w_o_na[i].T.astype(bf16)
        wout = w_out[i].astype(bf16)
        ng = norm_g[i].reshape(1, D_MODEL)
        qg = q_norm_g[i].reshape(HEAD_DIM, 1)
        kg = k_norm_g[i].reshape(HEAD_DIM, 1)
        table = _na_tables(na_rpb[i])
        last = i == depth - 1

        qta, k_a, vta, qtb, k_b, vtb = _proj_call(xr, ng, w1t, qg, kg, cos_r, sin_r, TOKEN_TILE)
        qta_m, k_a_m, vta_m, qtb_m, k_b_m, vtb_m = _proj_call(
            xm, ng, w1t, qg, kg, cos_m, sin_m, TOKEN_TILE)

        at_a = _gqa_call(qta, k_a, k_a_m, vta, vta_m, meta_queries=False)
        at_b = _na_call(qtb, k_b, k_b_m, vtb, vtb_m, table)
        xr_new = _out_call(xr, ng, at_a, at_b, w2t, woat, wobt, wout, fg, TOKEN_TILE, last)
        if not last:
            at_a_m = _gqa_call(qta_m, k_a, k_a_m, vta, vta_m, meta_queries=True)
            at_b_m = _na_meta_call(qtb_m, k_b_m, vtb_m, b)
            xm = _out_call(xm, ng, at_a_m, at_b_m, w2t, woat, wobt, wout, fg, TOKEN_TILE, False)
        xr = xr_new
    return xr
```

```python
import functools

import numpy as np
import jax
import jax.numpy as jnp
from jax import lax
from jax.experimental import pallas as pl
from jax.experimental.pallas import tpu as pltpu

D_MODEL = 1024
SEQ = 4096
GRID_W = 64
GRID_ROWS = SEQ // GRID_W
N_META = 16
META_PAD = 128
HEAD_DIM = 64
ATTN_HEADS = 8
ATTN_KV_HEADS = 2
ATTN_GROUP = ATTN_HEADS // ATTN_KV_HEADS
NA_HEADS = 8
NA_KH = 8
NA_KW = 16
ROPE_THETA = 10000.0
EPS = 1e-6
SCALE = HEAD_DIM ** -0.5
NEG = -1e30

ATTN_WIDTH = ATTN_HEADS * HEAD_DIM
KV_WIDTH = ATTN_KV_HEADS * HEAD_DIM
NA_WIDTH = NA_HEADS * HEAD_DIM

P1_QA, P1_KA, P1_VA, P1_QB, P1_KB, P1_VB, P1_END = 0, 512, 640, 768, 1280, 1792, 2304
P2_ZA, P2_ZB, P2_GA, P2_GB, P2_END = 0, 512, 1024, 2048, 3072

NA_QROWS = 4
NA_QBLK = NA_QROWS * GRID_W
NA_BAND_ROWS = 12
NA_BAND = NA_BAND_ROWS * GRID_W
NA_NBLK = GRID_ROWS // NA_QROWS

TOKEN_TILE = 512
GQA_TQ = 512
GQA_TK = 512
VMEM_LIMIT = 56 * 1024 * 1024

bf16 = jnp.bfloat16
f32 = jnp.float32


def _nn(a, b):
    return lax.dot_general(a, b, (((1,), (0,)), ((), ())), preferred_element_type=f32)


def _nt(a, b):
    return lax.dot_general(a, b, (((1,), (1,)), ((), ())), preferred_element_type=f32)


def _tn(a, b):
    return lax.dot_general(a, b, (((0,), (0,)), ((), ())), preferred_element_type=f32)


def _rms_rows(x, g):
    return x * lax.rsqrt(jnp.mean(x * x, axis=-1, keepdims=True) + EPS) * g


def _sigmoid(x):
    return 1.0 / (1.0 + jnp.exp(-x))


def _params(sem):
    return pltpu.CompilerParams(dimension_semantics=sem, vmem_limit_bytes=VMEM_LIMIT)


def _proj_kernel(x_ref, g_ref, w_ref, qg_ref, kg_ref, cos_ref, sin_ref,
                 qta_ref, ka_ref, vta_ref, qtb_ref, kb_ref, vtb_ref):
    h = _rms_rows(x_ref[0], g_ref[...]).astype(bf16)
    cos = cos_ref[...]
    sin = sin_ref[...]

    def proj(lo, hi):
        return _nt(w_ref[lo:hi, :], h)

    def norm_rope(t, gain):
        y = t * lax.rsqrt(jnp.mean(t * t, axis=0, keepdims=True) + EPS) * gain
        sw = jnp.concatenate([y[16:32], y[0:16], y[48:64], y[32:48]], axis=0)
        return y * cos + sw * sin

    qg = qg_ref[...]
    kg = kg_ref[...]
    qa = proj(P1_QA, P1_KA)
    for hd in range(ATTN_HEADS):
        lo = hd * HEAD_DIM
        qta_ref[0, lo:lo + HEAD_DIM, :] = (norm_rope(qa[lo:lo + HEAD_DIM], qg) * SCALE).astype(bf16)
    ka = proj(P1_KA, P1_VA)
    kt = jnp.concatenate([norm_rope(ka[0:HEAD_DIM], kg), norm_rope(ka[HEAD_DIM:], kg)], axis=0)
    ka_ref[0] = kt.T.astype(bf16)
    vta_ref[0] = proj(P1_VA, P1_QB).astype(bf16)
    qtb_ref[0] = (proj(P1_QB, P1_KB) * SCALE).astype(bf16)
    kbt = proj(P1_KB, P1_VB)
    for p in range(NA_HEADS // 2):
        kb_ref[0, p] = kbt[128 * p:128 * (p + 1)].T.astype(bf16)
    vtb_ref[0] = proj(P1_VB, P1_END).astype(bf16)


def _proj_call(x3, norm_g, w1t, qg, kg, cos_t, sin_t, tm):
    bx, lx, _ = x3.shape
    grid = (bx, lx // tm)
    out_shape = (
        jax.ShapeDtypeStruct((bx, ATTN_WIDTH, lx), bf16),
        jax.ShapeDtypeStruct((bx, lx, KV_WIDTH), bf16),
        jax.ShapeDtypeStruct((bx, KV_WIDTH, lx), bf16),
        jax.ShapeDtypeStruct((bx, NA_WIDTH, lx), bf16),
        jax.ShapeDtypeStruct((bx, NA_HEADS // 2, lx, 128), bf16),
        jax.ShapeDtypeStruct((bx, NA_WIDTH, lx), bf16),
    )
    return pl.pallas_call(
        _proj_kernel,
        grid=grid,
        in_specs=[
            pl.BlockSpec((1, tm, D_MODEL), lambda b, j: (b, j, 0)),
            pl.BlockSpec((1, D_MODEL), lambda b, j: (0, 0)),
            pl.BlockSpec((P1_END, D_MODEL), lambda b, j: (0, 0)),
            pl.BlockSpec((HEAD_DIM, 1), lambda b, j: (0, 0)),
            pl.BlockSpec((HEAD_DIM, 1), lambda b, j: (0, 0)),
            pl.BlockSpec((HEAD_DIM, tm), lambda b, j: (0, j)),
            pl.BlockSpec((HEAD_DIM, tm), lambda b, j: (0, j)),
        ],
        out_specs=(
            pl.BlockSpec((1, ATTN_WIDTH, tm), lambda b, j: (b, 0, j)),
            pl.BlockSpec((1, tm, KV_WIDTH), lambda b, j: (b, j, 0)),
            pl.BlockSpec((1, KV_WIDTH, tm), lambda b, j: (b, 0, j)),
            pl.BlockSpec((1, NA_WIDTH, tm), lambda b, j: (b, 0, j)),
            pl.BlockSpec((1, NA_HEADS // 2, tm, 128), lambda b, j: (b, 0, j, 0)),
            pl.BlockSpec((1, NA_WIDTH, tm), lambda b, j: (b, 0, j)),
        ),
        out_shape=out_shape,
        compiler_params=_params(("parallel", "parallel")),
        name="qkv_proj",
    )(x3, norm_g, w1t, qg, kg, cos_t, sin_t)


def _gqa_kernel(q_ref, k_ref, km_ref, vt_ref, vtm_ref, o_ref, *, tq, tk):
    g = pl.program_id(1)
    row = lax.broadcasted_iota(jnp.int32, (2 * HEAD_DIM, tq), 0)
    sel = (row >= HEAD_DIM) == (g == 1)
    meta_valid = row < N_META
    n_chunks = k_ref.shape[1] // tk

    def head(hh, carry):
        off = pl.multiple_of(hh * HEAD_DIM, HEAD_DIM)
        q = q_ref[0, pl.ds(off, HEAD_DIM), :]
        qext = jnp.where(sel, jnp.concatenate([q, q], axis=0), jnp.zeros((), bf16))
        s = jnp.where(meta_valid, _nn(km_ref[0], qext), NEG)
        m = jnp.max(s, axis=0, keepdims=True)
        p = jnp.exp(s - m)
        l = jnp.sum(p, axis=0, keepdims=True)
        acc = _nn(vtm_ref[0], p.astype(bf16))
        for c in range(n_chunks):
            s = _nn(k_ref[0, c * tk:(c + 1) * tk, :], qext)
            m_new = jnp.maximum(m, jnp.max(s, axis=0, keepdims=True))
            alpha = jnp.exp(m - m_new)
            p = jnp.exp(s - m_new)
            l = alpha * l + jnp.sum(p, axis=0, keepdims=True)
            acc = alpha * acc + _nn(vt_ref[0, :, c * tk:(c + 1) * tk], p.astype(bf16))
            m = m_new
        o_ref[0, pl.ds(off, HEAD_DIM), :] = (acc / l).astype(bf16)
        return carry

    lax.fori_loop(0, ATTN_GROUP, head, 0)


def _gqa_call(qt, k_real, k_meta, vt_real, vt_meta, *, meta_queries):
    b = k_real.shape[0]
    gw = ATTN_GROUP * HEAD_DIM
    if meta_queries:
        tq, nq = META_PAD, 1
        q_map = lambda bi, g, j: (0, g, bi)
    else:
        tq, nq = GQA_TQ, SEQ // GQA_TQ
        q_map = lambda bi, g, j: (bi, g, j)
    return pl.pallas_call(
        functools.partial(_gqa_kernel, tq=tq, tk=GQA_TK),
        grid=(b, ATTN_KV_HEADS, nq),
        in_specs=[
            pl.BlockSpec((1, gw, tq), q_map),
            pl.BlockSpec((1, SEQ, KV_WIDTH), lambda bi, g, j: (bi, 0, 0)),
            pl.BlockSpec((1, META_PAD, KV_WIDTH), lambda bi, g, j: (0, bi, 0)),
            pl.BlockSpec((1, HEAD_DIM, SEQ), lambda bi, g, j: (bi, g, 0)),
            pl.BlockSpec((1, HEAD_DIM, META_PAD), lambda bi, g, j: (0, g, bi)),
        ],
        out_specs=pl.BlockSpec((1, gw, tq), q_map),
        out_shape=jax.ShapeDtypeStruct(qt.shape, bf16),
        compiler_params=_params(("parallel", "parallel", "parallel")),
        name="gqa_meta" if meta_queries else "gqa_attn",
    )(qt, k_real, k_meta, vt_real, vt_meta)


def _na_block_geometry(blk):
    r0 = blk * NA_QROWS
    kb = min(max(r0 - NA_KH // 2, 0), GRID_ROWS - NA_BAND_ROWS)
    typ = 0 if blk == 0 else (2 if blk == NA_NBLK - 1 else 1)
    return r0, kb, typ


def _na_row_window(r):
    rs = min(max(r - NA_KH // 2, 0), GRID_ROWS - NA_KH)
    return rs, rs + NA_KH


def _na_build_tables(rpb_ref, h, t_ref, tab_ref):
    kc = lax.broadcasted_iota(jnp.int32, (GRID_W, 128), 0)
    lane = lax.broadcasted_iota(jnp.int32, (GRID_W, 128), 1)
    c = lane % GRID_W
    rel = kc - c + (NA_KW - 1)
    cs = jnp.clip(c - NA_KW // 2, 0, GRID_W - NA_KW)
    cvalid = (kc >= cs) & (kc < cs + NA_KW)
    neg_tile = jnp.full((GRID_W, 128), NEG, f32)

    def build_a(a, carry):
        acc = neg_tile
        for r in range(2 * NA_KW - 1):
            acc = jnp.where(rel == r, rpb_ref[h, a, r], acc)
        t_ref[a] = jnp.where(cvalid, acc, NEG)
        return carry

    lax.fori_loop(0, 2 * NA_KH - 1, build_a, 0)
    for typ, blk in enumerate((0, 1, NA_NBLK - 1)):
        r0, kb, _ = _na_block_geometry(blk)
        for i in range(NA_BAND_ROWS):
            kr = kb + i
            for jj in range(NA_QROWS // 2):
                halves = []
                for r in (r0 + 2 * jj, r0 + 2 * jj + 1):
                    lo, hi = _na_row_window(r)
                    halves.append(t_ref[kr - r + NA_KH - 1] if lo <= kr < hi else neg_tile)
                tab_ref[typ, i * GRID_W:(i + 1) * GRID_W, jj * 128:(jj + 1) * 128] = jnp.where(
                    lane < GRID_W, halves[0], halves[1])


def _na_kernel(rpb_ref, q_ref, k_ref, km_ref, vt_ref, vtm_ref, o_ref, t_ref, tab_ref):
    @pl.when(pl.program_id(1) == 0)
    def _():
        _na_build_tables(rpb_ref, pl.program_id(0), t_ref, tab_ref)

    par = pl.program_id(0) % 2
    row = lax.broadcasted_iota(jnp.int32, (2 * HEAD_DIM, NA_QBLK), 0)
    sel = (row >= HEAD_DIM) == (par == 1)
    meta_valid = row < N_META
    km = km_ref[0, 0]
    vtm = vtm_ref[0]
    for blk in range(NA_NBLK):
        _, kb, typ = _na_block_geometry(blk)
        ks = kb * GRID_W
        qs = blk * NA_QBLK
        q = q_ref[0, :, qs:qs + NA_QBLK]
        qext = jnp.where(sel, jnp.concatenate([q, q], axis=0), jnp.zeros((), bf16))
        s = _nn(k_ref[0, 0, ks:ks + NA_BAND, :], qext) + tab_ref[typ]
        sm = jnp.where(meta_valid, _nn(km, qext), NEG)
        m = jnp.maximum(jnp.max(s, axis=0, keepdims=True), jnp.max(sm, axis=0, keepdims=True))
        p = jnp.exp(s - m)
        pm = jnp.exp(sm - m)
        l = jnp.sum(p, axis=0, keepdims=True) + jnp.sum(pm, axis=0, keepdims=True)
        acc = _nn(vt_ref[0, :, ks:ks + NA_BAND], p.astype(bf16)) + _nn(vtm, pm.astype(bf16))
        o_ref[0, :, qs:qs + NA_QBLK] = (acc / l).astype(bf16)


def _na_call(qt, k_real, k_meta, vt_real, vt_meta, rpb):
    b = qt.shape[0]
    return pl.pallas_call(
        _na_kernel,
        grid=(NA_HEADS, b),
        in_specs=[
            pl.BlockSpec(memory_space=pltpu.SMEM),
            pl.BlockSpec((1, HEAD_DIM, SEQ), lambda h, bi: (bi, h, 0)),
            pl.BlockSpec((1, 1, SEQ, 128), lambda h, bi: (bi, h // 2, 0, 0)),
            pl.BlockSpec((1, 1, META_PAD, 128), lambda h, bi: (0, h // 2, bi, 0)),
            pl.BlockSpec((1, HEAD_DIM, SEQ), lambda h, bi: (bi, h, 0)),
            pl.BlockSpec((1, HEAD_DIM, META_PAD), lambda h, bi: (0, h, bi)),
        ],
        out_specs=pl.BlockSpec((1, HEAD_DIM, SEQ), lambda h, bi: (bi, h, 0)),
        out_shape=jax.ShapeDtypeStruct(qt.shape, bf16),
        scratch_shapes=[
            pltpu.VMEM((2 * NA_KH - 1, GRID_W, 128), f32),
            pltpu.VMEM((3, NA_BAND, NA_QBLK), f32),
        ],
        compiler_params=_params(("arbitrary", "arbitrary")),
        name="na_attn",
    )(rpb, qt, k_real, k_meta, vt_real, vt_meta)


def _na_meta_kernel(q_ref, km_ref, vtm_ref, o_ref):
    row = lax.broadcasted_iota(jnp.int32, (2 * HEAD_DIM, META_PAD), 0)
    meta_valid = row < N_META
    for h in range(NA_HEADS):
        lo = h * HEAD_DIM
        q = q_ref[0, lo:lo + HEAD_DIM, :]
        sel = (row >= HEAD_DIM) == (h % 2 == 1)
        qext = jnp.where(sel, jnp.concatenate([q, q], axis=0), jnp.zeros((), bf16))
        s = jnp.where(meta_valid, _nn(km_ref[0, h // 2], qext), NEG)
        m = jnp.max(s, axis=0, keepdims=True)
        p = jnp.exp(s - m)
        l = jnp.sum(p, axis=0, keepdims=True)
        acc = _nn(vtm_ref[0, lo:lo + HEAD_DIM, :], p.astype(bf16))
        o_ref[0, lo:lo + HEAD_DIM, :] = (acc / l).astype(bf16)


def _na_meta_call(qt_meta, k_meta, vt_meta, b):
    return pl.pallas_call(
        _na_meta_kernel,
        grid=(b,),
        in_specs=[
            pl.BlockSpec((1, NA_WIDTH, META_PAD), lambda bi: (0, 0, bi)),
            pl.BlockSpec((1, NA_HEADS // 2, META_PAD, 128), lambda bi: (0, 0, bi, 0)),
            pl.BlockSpec((1, NA_WIDTH, META_PAD), lambda bi: (0, 0, bi)),
        ],
        out_specs=pl.BlockSpec((1, NA_WIDTH, META_PAD), lambda bi: (0, 0, bi)),
        out_shape=jax.ShapeDtypeStruct(qt_meta.shape, bf16),
        compiler_params=_params(("parallel",)),
        name="na_meta",
    )(qt_meta, k_meta, vt_meta)


def _out_kernel(x_ref, g_ref, ata_ref, atb_ref, w2_ref, woa_ref, wob_ref, wout_ref, fg_ref,
                o_ref, *, final):
    x = x_ref[0]
    h = _rms_rows(x, g_ref[...]).astype(bf16)

    def gated(at_ref, lo, hi):
        z = _nt(w2_ref[lo:hi, :], h)
        return (at_ref[0].astype(f32) * (z * _sigmoid(z))).astype(bf16)

    ya = _nn(woa_ref[...], gated(ata_ref, P2_ZA, P2_ZB))
    yb = _nn(wob_ref[...], gated(atb_ref, P2_ZB, P2_GA))
    ga = _nt(w2_ref[P2_GA:P2_GB, :], h)
    gb = _nt(w2_ref[P2_GB:P2_END, :], h)
    mix = (_sigmoid(ga) * ya + _sigmoid(gb) * yb).astype(bf16)
    out = x + _tn(mix, wout_ref[...])
    if final:
        out = _rms_rows(out, fg_ref[...])
    o_ref[0] = out


def _out_call(x3, norm_g, at_a, at_b, w2t, woat, wobt, wout, final_g, tm, final):
    bx, lx, _ = x3.shape
    const = lambda b, j: (0, 0)
    return pl.pallas_call(
        functools.partial(_out_kernel, final=final),
        grid=(bx, lx // tm),
        in_specs=[
            pl.BlockSpec((1, tm, D_MODEL), lambda b, j: (b, j, 0)),
            pl.BlockSpec((1, D_MODEL), const),
            pl.BlockSpec((1, ATTN_WIDTH, tm), lambda b, j: (b, 0, j)),
            pl.BlockSpec((1, NA_WIDTH, tm), lambda b, j: (b, 0, j)),
            pl.BlockSpec((P2_END, D_MODEL), const),
            pl.BlockSpec((D_MODEL, ATTN_WIDTH), const),
            pl.BlockSpec((D_MODEL, NA_WIDTH), const),
            pl.BlockSpec((D_MODEL, D_MODEL), const),
            pl.BlockSpec((1, D_MODEL), const),
        ],
        out_specs=pl.BlockSpec((1, tm, D_MODEL), lambda b, j: (b, j, 0)),
        out_shape=jax.ShapeDtypeStruct(x3.shape, f32),
        compiler_params=_params(("parallel", "parallel")),
        name="gate_out_proj",
    )(x3, norm_g, at_a, at_b, w2t, woat, wobt, wout, final_g)


def _rope_tables():
    t = np.arange(SEQ)
    axis_dim = HEAD_DIM // 2
    inv = jnp.asarray(ROPE_THETA, f32) ** (-jnp.arange(0, axis_dim, 2, dtype=f32) / axis_dim)
    ang_r = jnp.asarray(t // GRID_W, f32)[None, :] * inv[:, None]
    ang_c = jnp.asarray(t % GRID_W, f32)[None, :] * inv[:, None]
    ang = jnp.concatenate([ang_r, ang_r, ang_c, ang_c], axis=0)
    sign = np.repeat(np.array([-1.0, 1.0, -1.0, 1.0], np.float32), axis_dim // 2)[:, None]
    return jnp.cos(ang), jnp.sin(ang) * sign


def kernel(x, meta_tokens, norm_g, w_in, q_norm_g, k_norm_g, na_rpb, w_o_attn, w_o_na, w_out,
           final_norm_g):
    b = x.shape[0]
    depth = w_in.shape[0]
    cos_r, sin_r = _rope_tables()
    cos_m = jnp.ones((HEAD_DIM, b * META_PAD), f32)
    sin_m = jnp.zeros((HEAD_DIM, b * META_PAD), f32)

    meta_pad = jnp.concatenate(
        [meta_tokens.astype(f32), jnp.zeros((META_PAD - N_META, D_MODEL), f32)], axis=0)
    xm = jnp.tile(meta_pad, (b, 1))[None]
    xr = x
    fg = final_norm_g.reshape(1, D_MODEL)

    for i in range(depth):
        wi = w_in[i]
        qa, ka, va, za, qb, kb, vb, zb, ga, gb = (
            wi[:, 0:512], wi[:, 512:640], wi[:, 640:768], wi[:, 768:1280], wi[:, 1280:1792],
            wi[:, 1792:2304], wi[:, 2304:2816], wi[:, 2816:3328], wi[:, 3328:4352], wi[:, 4352:5376])
        w1t = jnp.concatenate([qa, ka, va, qb, kb, vb], axis=1).T.astype(bf16)
        w2t = jnp.concatenate([za, zb, ga, gb], axis=1).T.astype(bf16)
        woat = w_o_attn[i].T.astype(bf16)
        wobt = w_o_na[i].T.astype(bf16)
        wout = w_out[i].astype(bf16)
        ng = norm_g[i].reshape(1, D_MODEL)
        qg = q_norm_g[i].reshape(HEAD_DIM, 1)
        kg = k_norm_g[i].reshape(HEAD_DIM, 1)
        last = i == depth - 1

        qta, k_a, vta, qtb, k_b, vtb = _proj_call(xr, ng, w1t, qg, kg, cos_r, sin_r, TOKEN_TILE)
        qta_m, k_a_m, vta_m, qtb_m, k_b_m, vtb_m = _proj_call(
            xm, ng, w1t, qg, kg, cos_m, sin_m, TOKEN_TILE)

        at_a = _gqa_call(qta, k_a, k_a_m, vta, vta_m, meta_queries=False)
        at_b = _na_call(qtb, k_b, k_b_m, vtb, vtb_m, na_rpb[i])
        xr_new = _out_call(xr, ng, at_a, at_b, w2t, woat, wobt, wout, fg, TOKEN_TILE, last)
        if not last:
            at_a_m = _gqa_call(qta_m, k_a, k_a_m, vta, vta_m, meta_queries=True)
            at_b_m = _na_meta_call(qtb_m, k_b_m, vtb_m, b)
            xm = _out_call(xm, ng, at_a_m, at_b_m, w2t, woat, wobt, wout, fg, TOKEN_TILE, False)
        xr = xr_new
    return xr
```

```python
import functools

import numpy as np
import jax
import jax.numpy as jnp
from jax import lax
from jax.experimental import pallas as pl
from jax.experimental.pallas import tpu as pltpu

D_MODEL = 1024
SEQ = 4096
GRID_W = 64
GRID_ROWS = SEQ // GRID_W
N_META = 16
META_PAD = 128
HEAD_DIM = 64
ATTN_HEADS = 8
ATTN_KV_HEADS = 2
ATTN_GROUP = ATTN_HEADS // ATTN_KV_HEADS
NA_HEADS = 8
NA_KH = 8
NA_KW = 16
ROPE_THETA = 10000.0
EPS = 1e-6
SCALE = HEAD_DIM ** -0.5
LOG2E = 1.4426950408889634
Q_SCALE = SCALE * LOG2E
NEG = -1e30

ATTN_WIDTH = ATTN_HEADS * HEAD_DIM
KV_WIDTH = ATTN_KV_HEADS * HEAD_DIM
NA_WIDTH = NA_HEADS * HEAD_DIM

P1_QA, P1_KA, P1_VA, P1_QB, P1_KB, P1_VB, P1_END = 0, 512, 640, 768, 1280, 1792, 2304
P2_ZA, P2_ZB, P2_GA, P2_GB, P2_END = 0, 512, 1024, 2048, 3072

NA_QROWS = 4
NA_QBLK = NA_QROWS * GRID_W
NA_BAND_ROWS = 12
NA_BAND = NA_BAND_ROWS * GRID_W
NA_NBLK = GRID_ROWS // NA_QROWS

TOKEN_TILE = 512
GQA_TK = 512
VMEM_LIMIT = 56 * 1024 * 1024

bf16 = jnp.bfloat16
f32 = jnp.float32


def _nn(a, b):
    return lax.dot_general(a, b, (((1,), (0,)), ((), ())), preferred_element_type=f32)


def _nt(a, b):
    return lax.dot_general(a, b, (((1,), (1,)), ((), ())), preferred_element_type=f32)


def _tn(a, b):
    return lax.dot_general(a, b, (((0,), (0,)), ((), ())), preferred_element_type=f32)


def _rms_rows(x, g):
    return x * lax.rsqrt(jnp.mean(x * x, axis=-1, keepdims=True) + EPS) * g


def _sigmoid(x):
    return 1.0 / (1.0 + jnp.exp(-x))


def _params(sem):
    return pltpu.CompilerParams(dimension_semantics=sem, vmem_limit_bytes=VMEM_LIMIT)


def _proj_kernel(x_ref, g_ref, w_ref, qg_ref, kg_ref, cos_ref, sin_ref,
                 qta_ref, ka_ref, vta_ref, qtb_ref, kb_ref, vtb_ref):
    h = _rms_rows(x_ref[0], g_ref[...]).astype(bf16)
    cos = cos_ref[...]
    sin = sin_ref[...]

    def proj(lo, hi):
        return _nt(w_ref[lo:hi, :], h)

    def norm_rope(t, gain):
        y = t * lax.rsqrt(jnp.mean(t * t, axis=0, keepdims=True) + EPS) * gain
        sw = jnp.concatenate([y[16:32], y[0:16], y[48:64], y[32:48]], axis=0)
        return y * cos + sw * sin

    qg = qg_ref[...]
    kg = kg_ref[...]
    qa = proj(P1_QA, P1_KA)
    for hd in range(ATTN_HEADS):
        lo = hd * HEAD_DIM
        qta_ref[0, 0, lo:lo + HEAD_DIM, :] = (norm_rope(qa[lo:lo + HEAD_DIM], qg) * Q_SCALE).astype(bf16)
    ka = proj(P1_KA, P1_VA)
    kt = jnp.concatenate([norm_rope(ka[0:HEAD_DIM], kg), norm_rope(ka[HEAD_DIM:], kg)], axis=0)
    ka_ref[0] = kt.T.astype(bf16)
    vta_ref[0] = proj(P1_VA, P1_QB).astype(bf16)
    qtb_ref[0] = (proj(P1_QB, P1_KB) * Q_SCALE).astype(bf16)
    kbt = proj(P1_KB, P1_VB)
    for p in range(NA_HEADS // 2):
        kb_ref[0, p] = kbt[128 * p:128 * (p + 1)].T.astype(bf16)
    vtb_ref[0] = proj(P1_VB, P1_END).astype(bf16)


def _proj_call(x3, norm_g, w1t, qg, kg, cos_t, sin_t, tm):
    bx, lx, _ = x3.shape
    grid = (bx, lx // tm)
    out_shape = (
        jax.ShapeDtypeStruct((bx, lx // tm, ATTN_WIDTH, tm), bf16),
        jax.ShapeDtypeStruct((bx, lx, KV_WIDTH), bf16),
        jax.ShapeDtypeStruct((bx, KV_WIDTH, lx), bf16),
        jax.ShapeDtypeStruct((bx, NA_WIDTH, lx), bf16),
        jax.ShapeDtypeStruct((bx, NA_HEADS // 2, lx, 128), bf16),
        jax.ShapeDtypeStruct((bx, NA_WIDTH, lx), bf16),
    )
    return pl.pallas_call(
        _proj_kernel,
        grid=grid,
        in_specs=[
            pl.BlockSpec((1, tm, D_MODEL), lambda b, j: (b, j, 0)),
            pl.BlockSpec((1, D_MODEL), lambda b, j: (0, 0)),
            pl.BlockSpec((P1_END, D_MODEL), lambda b, j: (0, 0)),
            pl.BlockSpec((HEAD_DIM, 1), lambda b, j: (0, 0)),
            pl.BlockSpec((HEAD_DIM, 1), lambda b, j: (0, 0)),
            pl.BlockSpec((HEAD_DIM, tm), lambda b, j: (0, j)),
            pl.BlockSpec((HEAD_DIM, tm), lambda b, j: (0, j)),
        ],
        out_specs=(
            pl.BlockSpec((1, 1, ATTN_WIDTH, tm), lambda b, j: (b, j, 0, 0)),
            pl.BlockSpec((1, tm, KV_WIDTH), lambda b, j: (b, j, 0)),
            pl.BlockSpec((1, KV_WIDTH, tm), lambda b, j: (b, 0, j)),
            pl.BlockSpec((1, NA_WIDTH, tm), lambda b, j: (b, 0, j)),
            pl.BlockSpec((1, NA_HEADS // 2, tm, 128), lambda b, j: (b, 0, j, 0)),
            pl.BlockSpec((1, NA_WIDTH, tm), lambda b, j: (b, 0, j)),
        ),
        out_shape=out_shape,
        compiler_params=_params(("parallel", "parallel")),
        name="qkv_proj",
    )(x3, norm_g, w1t, qg, kg, cos_t, sin_t)


def _colmax8(x):
    return jnp.max(x.reshape(x.shape[0] // 8, 8, x.shape[1]), axis=0)


def _colsum8(x):
    return jnp.sum(x.reshape(x.shape[0] // 8, 8, x.shape[1]), axis=0)


def _gqa_kernel(q_ref, k_ref, km_ref, vt_ref, vtm_ref, o_ref, s0_ref, s1_ref, p_ref, m_ref,
                *, tq, n_qblk):
    g = pl.program_id(1)
    row = lax.broadcasted_iota(jnp.int32, (2 * HEAD_DIM, tq), 0)
    sel = (row >= HEAD_DIM) == (g == 1)
    meta_valid = row < N_META
    n_items = n_qblk * ATTN_GROUP
    n_chunks = SEQ // GQA_TK
    blocks = [(0, META_PAD)] + [(META_PAD + c * GQA_TK, GQA_TK) for c in range(n_chunks)]

    def item_index(t):
        return t // ATTN_GROUP, pl.multiple_of((t % ATTN_GROUP) * HEAD_DIM, HEAD_DIM)

    def scores(t, s_ref, slot):
        qi, off = item_index(t)
        q = q_ref[0, qi, pl.ds(off, HEAD_DIM), :]
        qext = jnp.where(sel, jnp.concatenate([q, q], axis=0), jnp.zeros((), bf16))
        s = jnp.where(meta_valid, _nn(km_ref[0], qext), NEG)
        s_ref[0:META_PAD, :] = s
        m8 = _colmax8(s)
        for c in range(n_chunks):
            s = _nn(k_ref[0, c * GQA_TK:(c + 1) * GQA_TK, :], qext)
            lo = META_PAD + c * GQA_TK
            s_ref[lo:lo + GQA_TK, :] = s
            m8 = jnp.maximum(m8, _colmax8(s))
        m_ref[slot] = jnp.max(m8, axis=0, keepdims=True)

    def finish(t, s_ref, slot):
        qi, off = item_index(t)
        m = m_ref[slot]
        l8 = jnp.zeros((8, tq), f32)
        for lo, n in blocks:
            p = jnp.exp2(s_ref[lo:lo + n, :] - m)
            l8 = l8 + _colsum8(p)
            p_ref[lo:lo + n, :] = p.astype(bf16)
        acc = _nn(vtm_ref[0], p_ref[0:META_PAD, :]) + _nn(vt_ref[0], p_ref[META_PAD:, :])
        l = jnp.sum(l8, axis=0, keepdims=True)
        o_ref[0, qi, pl.ds(off, HEAD_DIM), :] = (acc / l).astype(bf16)

    scores(0, s0_ref, 0)

    def pair(u, carry):
        t = 2 * u
        scores(t + 1, s1_ref, 1)
        finish(t, s0_ref, 0)
        scores(jnp.minimum(t + 2, n_items - 1), s0_ref, 0)
        finish(t + 1, s1_ref, 1)
        return carry

    lax.fori_loop(0, n_items // 2, pair, 0)


def _gqa_call(qt, k_real, k_meta, vt_real, vt_meta, *, meta_queries):
    b = k_real.shape[0]
    gw = ATTN_GROUP * HEAD_DIM
    if meta_queries:
        per_tile = TOKEN_TILE // META_PAD
        tq, n_qblk = META_PAD, 1
        q_map = lambda bi, g: (0, bi // per_tile, g, bi % per_tile)
    else:
        tq, n_qblk = TOKEN_TILE, SEQ // TOKEN_TILE
        q_map = lambda bi, g: (bi, 0, g, 0)
    n_keys = META_PAD + SEQ
    return pl.pallas_call(
        functools.partial(_gqa_kernel, tq=tq, n_qblk=n_qblk),
        grid=(b, ATTN_KV_HEADS),
        in_specs=[
            pl.BlockSpec((1, n_qblk, gw, tq), q_map),
            pl.BlockSpec((1, SEQ, KV_WIDTH), lambda bi, g: (bi, 0, 0)),
            pl.BlockSpec((1, META_PAD, KV_WIDTH), lambda bi, g: (0, bi, 0)),
            pl.BlockSpec((1, HEAD_DIM, SEQ), lambda bi, g: (bi, g, 0)),
            pl.BlockSpec((1, HEAD_DIM, META_PAD), lambda bi, g: (0, g, bi)),
        ],
        out_specs=pl.BlockSpec((1, n_qblk, gw, tq), q_map),
        out_shape=jax.ShapeDtypeStruct(qt.shape, bf16),
        scratch_shapes=[
            pltpu.VMEM((n_keys, tq), f32),
            pltpu.VMEM((n_keys, tq), f32),
            pltpu.VMEM((n_keys, tq), bf16),
            pltpu.VMEM((2, 1, tq), f32),
        ],
        compiler_params=_params(("parallel", "parallel")),
        name="gqa_meta" if meta_queries else "gqa_attn",
    )(qt, k_real, k_meta, vt_real, vt_meta)


def _na_block_geometry(blk):
    r0 = blk * NA_QROWS
    kb = min(max(r0 - NA_KH // 2, 0), GRID_ROWS - NA_BAND_ROWS)
    typ = 0 if blk == 0 else (2 if blk == NA_NBLK - 1 else 1)
    return r0, kb, typ


def _na_row_window(r):
    rs = min(max(r - NA_KH // 2, 0), GRID_ROWS - NA_KH)
    return rs, rs + NA_KH


def _na_build_tables(rpb_ref, h, t_ref, tab_ref):
    kc = lax.broadcasted_iota(jnp.int32, (GRID_W, 128), 0)
    lane = lax.broadcasted_iota(jnp.int32, (GRID_W, 128), 1)
    c = lane % GRID_W
    rel = kc - c + (NA_KW - 1)
    cs = jnp.clip(c - NA_KW // 2, 0, GRID_W - NA_KW)
    cvalid = (kc >= cs) & (kc < cs + NA_KW)
    neg_tile = jnp.full((GRID_W, 128), NEG, f32)

    def build_a(a, carry):
        acc = neg_tile
        for r in range(2 * NA_KW - 1):
            acc = jnp.where(rel == r, rpb_ref[h, a, r] * LOG2E, acc)
        t_ref[a] = jnp.where(cvalid, acc, NEG)
        return carry

    lax.fori_loop(0, 2 * NA_KH - 1, build_a, 0)
    for typ, blk in enumerate((0, 1, NA_NBLK - 1)):
        r0, kb, _ = _na_block_geometry(blk)
        for i in range(NA_BAND_ROWS):
            kr = kb + i
            for jj in range(NA_QROWS // 2):
                halves = []
                for r in (r0 + 2 * jj, r0 + 2 * jj + 1):
                    lo, hi = _na_row_window(r)
                    halves.append(t_ref[kr - r + NA_KH - 1] if lo <= kr < hi else neg_tile)
                tab_ref[typ, i * GRID_W:(i + 1) * GRID_W, jj * 128:(jj + 1) * 128] = jnp.where(
                    lane < GRID_W, halves[0], halves[1])


def _na_kernel(rpb_ref, q_ref, k_ref, km_ref, vt_ref, vtm_ref, o_ref, t_ref, tab_ref):
    @pl.when(pl.program_id(1) == 0)
    def _():
        _na_build_tables(rpb_ref, pl.program_id(0), t_ref, tab_ref)

    par = pl.program_id(0) % 2
    row = lax.broadcasted_iota(jnp.int32, (2 * HEAD_DIM, NA_QBLK), 0)
    sel = (row >= HEAD_DIM) == (par == 1)
    meta_valid = row < N_META
    km = km_ref[0, 0]
    vtm = vtm_ref[0]
    for blk in range(NA_NBLK):
        _, kb, typ = _na_block_geometry(blk)
        ks = kb * GRID_W
        qs = blk * NA_QBLK
        q = q_ref[0, :, qs:qs + NA_QBLK]
        qext = jnp.where(sel, jnp.concatenate([q, q], axis=0), jnp.zeros((), bf16))
        s = _nn(k_ref[0, 0, ks:ks + NA_BAND, :], qext) + tab_ref[typ]
        sm = jnp.where(meta_valid, _nn(km, qext), NEG)
        m = jnp.maximum(jnp.max(s, axis=0, keepdims=True), jnp.max(sm, axis=0, keepdims=True))
        p = jnp.exp2(s - m)
        pm = jnp.exp2(sm - m)
        l = jnp.sum(p, axis=0, keepdims=True) + jnp.sum(pm, axis=0, keepdims=True)
        acc = _nn(vt_ref[0, :, ks:ks + NA_BAND], p.astype(bf16)) + _nn(vtm, pm.astype(bf16))
        o_ref[0, :, qs:qs + NA_QBLK] = (acc / l).astype(bf16)


def _na_call(qt, k_real, k_meta, vt_real, vt_meta, rpb):
    b = qt.shape[0]
    return pl.pallas_call(
        _na_kernel,
        grid=(NA_HEADS, b),
        in_specs=[
            pl.BlockSpec(memory_space=pltpu.SMEM),
            pl.BlockSpec((1, HEAD_DIM, SEQ), lambda h, bi: (bi, h, 0)),
            pl.BlockSpec((1, 1, SEQ, 128), lambda h, bi: (bi, h // 2, 0, 0)),
            pl.BlockSpec((1, 1, META_PAD, 128), lambda h, bi: (0, h // 2, bi, 0)),
            pl.BlockSpec((1, HEAD_DIM, SEQ), lambda h, bi: (bi, h, 0)),
            pl.BlockSpec((1, HEAD_DIM, META_PAD), lambda h, bi: (0, h, bi)),
        ],
        out_specs=pl.BlockSpec((1, HEAD_DIM, SEQ), lambda h, bi: (bi, h, 0)),
        out_shape=jax.ShapeDtypeStruct(qt.shape, bf16),
        scratch_shapes=[
            pltpu.VMEM((2 * NA_KH - 1, GRID_W, 128), f32),
            pltpu.VMEM((3, NA_BAND, NA_QBLK), f32),
        ],
        compiler_params=_params(("arbitrary", "arbitrary")),
        name="na_attn",
    )(rpb, qt, k_real, k_meta, vt_real, vt_meta)


def _na_meta_kernel(q_ref, km_ref, vtm_ref, o_ref):
    row = lax.broadcasted_iota(jnp.int32, (2 * HEAD_DIM, META_PAD), 0)
    meta_valid = row < N_META
    for h in range(NA_HEADS):
        lo = h * HEAD_DIM
        q = q_ref[0, lo:lo + HEAD_DIM, :]
        sel = (row >= HEAD_DIM) == (h % 2 == 1)
        qext = jnp.where(sel, jnp.concatenate([q, q], axis=0), jnp.zeros((), bf16))
        s = jnp.where(meta_valid, _nn(km_ref[0, h // 2], qext), NEG)
        m = jnp.max(s, axis=0, keepdims=True)
        p = jnp.exp2(s - m)
        l = jnp.sum(p, axis=0, keepdims=True)
        acc = _nn(vtm_ref[0, lo:lo + HEAD_DIM, :], p.astype(bf16))
        o_ref[0, lo:lo + HEAD_DIM, :] = (acc / l).astype(bf16)


def _na_meta_call(qt_meta, k_meta, vt_meta, b):
    return pl.pallas_call(
        _na_meta_kernel,
        grid=(b,),
        in_specs=[
            pl.BlockSpec((1, NA_WIDTH, META_PAD), lambda bi: (0, 0, bi)),
            pl.BlockSpec((1, NA_HEADS // 2, META_PAD, 128), lambda bi: (0, 0, bi, 0)),
            pl.BlockSpec((1, NA_WIDTH, META_PAD), lambda bi: (0, 0, bi)),
        ],
        out_specs=pl.BlockSpec((1, NA_WIDTH, META_PAD), lambda bi: (0, 0, bi)),
        out_shape=jax.ShapeDtypeStruct(qt_meta.shape, bf16),
        compiler_params=_params(("parallel",)),
        name="na_meta",
    )(qt_meta, k_meta, vt_meta)


def _out_kernel(x_ref, g_ref, ata_ref, atb_ref, w2_ref, woa_ref, wob_ref, wout_ref, fg_ref,
                o_ref, *, final):
    x = x_ref[0]
    h = _rms_rows(x, g_ref[...]).astype(bf16)

    def gated(at_ref, lo, hi):
        z = _nt(w2_ref[lo:hi, :], h)
        return (at_ref[...].reshape(z.shape).astype(f32) * (z * _sigmoid(z))).astype(bf16)

    ya = _nn(woa_ref[...], gated(ata_ref, P2_ZA, P2_ZB))
    yb = _nn(wob_ref[...], gated(atb_ref, P2_ZB, P2_GA))
    ga = _nt(w2_ref[P2_GA:P2_GB, :], h)
    gb = _nt(w2_ref[P2_GB:P2_END, :], h)
    mix = (_sigmoid(ga) * ya + _sigmoid(gb) * yb).astype(bf16)
    out = x + _tn(mix, wout_ref[...])
    if final:
        out = _rms_rows(out, fg_ref[...])
    o_ref[0] = out


def _out_call(x3, norm_g, at_a, at_b, w2t, woat, wobt, wout, final_g, tm, final):
    bx, lx, _ = x3.shape
    const = lambda b, j: (0, 0)
    return pl.pallas_call(
        functools.partial(_out_kernel, final=final),
        grid=(bx, lx // tm),
        in_specs=[
            pl.BlockSpec((1, tm, D_MODEL), lambda b, j: (b, j, 0)),
            pl.BlockSpec((1, D_MODEL), const),
            pl.BlockSpec((1, 1, ATTN_WIDTH, tm), lambda b, j: (b, j, 0, 0)),
            pl.BlockSpec((1, NA_WIDTH, tm), lambda b, j: (b, 0, j)),
            pl.BlockSpec((P2_END, D_MODEL), const),
            pl.BlockSpec((D_MODEL, ATTN_WIDTH), const),
            pl.BlockSpec((D_MODEL, NA_WIDTH), const),
            pl.BlockSpec((D_MODEL, D_MODEL), const),
            pl.BlockSpec((1, D_MODEL), const),
        ],
        out_specs=pl.BlockSpec((1, tm, D_MODEL), lambda b, j: (b, j, 0)),
        out_shape=jax.ShapeDtypeStruct(x3.shape, f32),
        compiler_params=_params(("parallel", "parallel")),
        name="gate_out_proj",
    )(x3, norm_g, at_a, at_b, w2t, woat, wobt, wout, final_g)


def _rope_tables():
    t = np.arange(SEQ)
    axis_dim = HEAD_DIM // 2
    inv = jnp.asarray(ROPE_THETA, f32) ** (-jnp.arange(0, axis_dim, 2, dtype=f32) / axis_dim)
    ang_r = jnp.asarray(t // GRID_W, f32)[None, :] * inv[:, None]
    ang_c = jnp.asarray(t % GRID_W, f32)[None, :] * inv[:, None]
    ang = jnp.concatenate([ang_r, ang_r, ang_c, ang_c], axis=0)
    sign = np.repeat(np.array([-1.0, 1.0, -1.0, 1.0], np.float32), axis_dim // 2)[:, None]
    return jnp.cos(ang), jnp.sin(ang) * sign


def kernel(x, meta_tokens, norm_g, w_in, q_norm_g, k_norm_g, na_rpb, w_o_attn, w_o_na, w_out,
           final_norm_g):
    b = x.shape[0]
    depth = w_in.shape[0]
    cos_r, sin_r = _rope_tables()
    cos_m = jnp.ones((HEAD_DIM, b * META_PAD), f32)
    sin_m = jnp.zeros((HEAD_DIM, b * META_PAD), f32)

    meta_pad = jnp.concatenate(
        [meta_tokens.astype(f32), jnp.zeros((META_PAD - N_META, D_MODEL), f32)], axis=0)
    xm = jnp.tile(meta_pad, (b, 1))[None]
    xr = x
    fg = final_norm_g.reshape(1, D_MODEL)

    for i in range(depth):
        wi = w_in[i]
        qa, ka, va, za, qb, kb, vb, zb, ga, gb = (
            wi[:, 0:512], wi[:, 512:640], wi[:, 640:768], wi[:, 768:1280], wi[:, 1280:1792],
            wi[:, 1792:2304], wi[:, 2304:2816], wi[:, 2816:3328], wi[:, 3328:4352], wi[:, 4352:5376])
        w1t = jnp.concatenate([qa, ka, va, qb, kb, vb], axis=1).T.astype(bf16)
        w2t = jnp.concatenate([za, zb, ga, gb], axis=1).T.astype(bf16)
        woat = w_o_attn[i].T.astype(bf16)
        wobt = w_o_na[i].T.astype(bf16)
        wout = w_out[i].astype(bf16)
        ng = norm_g[i].reshape(1, D_MODEL)
        qg = q_norm_g[i].reshape(HEAD_DIM, 1)
        kg = k_norm_g[i].reshape(HEAD_DIM, 1)
        last = i == depth - 1

        qta, k_a, vta, qtb, k_b, vtb = _proj_call(xr, ng, w1t, qg, kg, cos_r, sin_r, TOKEN_TILE)
        qta_m, k_a_m, vta_m, qtb_m, k_b_m, vtb_m = _proj_call(
            xm, ng, w1t, qg, kg, cos_m, sin_m, TOKEN_TILE)

        at_a = _gqa_call(qta, k_a, k_a_m, vta, vta_m, meta_queries=False)
        at_b = _na_call(qtb, k_b, k_b_m, vtb, vtb_m, na_rpb[i])
        xr_new = _out_call(xr, ng, at_a, at_b, w2t, woat, wobt, wout, fg, TOKEN_TILE, last)
        if not last:
            at_a_m = _gqa_call(qta_m, k_a, k_a_m, vta, vta_m, meta_queries=True)
            at_b_m = _na_meta_call(qtb_m, k_b_m, vtb_m, b)
            xm = _out_call(xm, ng, at_a_m, at_b_m, w2t, woat, wobt, wout, fg, TOKEN_TILE, False)
        xr = xr_new
    return xr
```

```python
import functools

import numpy as np
import jax
import jax.numpy as jnp
from jax import lax
from jax.experimental import pallas as pl
from jax.experimental.pallas import tpu as pltpu

D_MODEL = 1024
SEQ = 4096
GRID_W = 64
GRID_ROWS = SEQ // GRID_W
N_META = 16
META_PAD = 128
HEAD_DIM = 64
ATTN_HEADS = 8
ATTN_KV_HEADS = 2
ATTN_GROUP = ATTN_HEADS // ATTN_KV_HEADS
NA_HEADS = 8
NA_KH = 8
NA_KW = 16
ROPE_THETA = 10000.0
EPS = 1e-6
SCALE = HEAD_DIM ** -0.5
LOG2E = 1.4426950408889634
Q_SCALE = SCALE * LOG2E
NEG = -1e30

ATTN_WIDTH = ATTN_HEADS * HEAD_DIM
KV_WIDTH = ATTN_KV_HEADS * HEAD_DIM
NA_WIDTH = NA_HEADS * HEAD_DIM

P1_QA, P1_KA, P1_VA, P1_QB, P1_KB, P1_VB, P1_END = 0, 512, 640, 768, 1280, 1792, 2304
P2_ZA, P2_ZB, P2_GA, P2_GB, P2_END = 0, 512, 1024, 2048, 3072

NA_QROWS = 4
NA_QBLK = NA_QROWS * GRID_W
NA_BAND_ROWS = 12
NA_BAND = NA_BAND_ROWS * GRID_W
NA_NBLK = GRID_ROWS // NA_QROWS
NA_STAGE_BLKS = 2

TOKEN_TILE = 512
GQA_TK = 512
GQA_ONES_ROWS = 16
VMEM_LIMIT = 56 * 1024 * 1024

bf16 = jnp.bfloat16
f32 = jnp.float32


def _nn(a, b):
    return lax.dot_general(a, b, (((1,), (0,)), ((), ())), preferred_element_type=f32)


def _nt(a, b):
    return lax.dot_general(a, b, (((1,), (1,)), ((), ())), preferred_element_type=f32)


def _tn(a, b):
    return lax.dot_general(a, b, (((0,), (0,)), ((), ())), preferred_element_type=f32)


def _rms_rows(x, g):
    return x * lax.rsqrt(jnp.mean(x * x, axis=-1, keepdims=True) + EPS) * g


def _sigmoid(x):
    return 1.0 / (1.0 + jnp.exp(-x))


def _params(sem, flags=None):
    return pltpu.CompilerParams(dimension_semantics=sem, vmem_limit_bytes=VMEM_LIMIT, flags=flags)


def _proj_kernel(x_ref, g_ref, w_ref, qg_ref, kg_ref, cos_ref, sin_ref,
                 qta_ref, ka_ref, vta_ref, qtb_ref, kb_ref, vtb_ref):
    h = _rms_rows(x_ref[0], g_ref[...]).astype(bf16)
    cos = cos_ref[...]
    sin = sin_ref[...]

    def proj(lo, hi):
        return _nt(w_ref[lo:hi, :], h)

    def norm_rope(t, gain):
        y = t * lax.rsqrt(jnp.mean(t * t, axis=0, keepdims=True) + EPS) * gain
        sw = jnp.concatenate([y[16:32], y[0:16], y[48:64], y[32:48]], axis=0)
        return y * cos + sw * sin

    qg = qg_ref[...]
    kg = kg_ref[...]
    qa = proj(P1_QA, P1_KA)
    for hd in range(ATTN_HEADS):
        lo = hd * HEAD_DIM
        qta_ref[0, 0, lo:lo + HEAD_DIM, :] = (norm_rope(qa[lo:lo + HEAD_DIM], qg) * Q_SCALE).astype(bf16)
    ka = proj(P1_KA, P1_VA)
    kt = jnp.concatenate([norm_rope(ka[0:HEAD_DIM], kg), norm_rope(ka[HEAD_DIM:], kg)], axis=0)
    ka_ref[0] = kt.T.astype(bf16)
    vta_ref[0] = proj(P1_VA, P1_QB).astype(bf16)
    qtb_ref[0] = (proj(P1_QB, P1_KB) * Q_SCALE).astype(bf16)
    kbt = proj(P1_KB, P1_VB)
    for p in range(NA_HEADS // 2):
        kb_ref[0, p] = kbt[128 * p:128 * (p + 1)].T.astype(bf16)
    vtb_ref[0] = proj(P1_VB, P1_END).astype(bf16)


def _proj_call(x3, norm_g, w1t, qg, kg, cos_t, sin_t, tm):
    bx, lx, _ = x3.shape
    grid = (bx, lx // tm)
    out_shape = (
        jax.ShapeDtypeStruct((bx, lx // tm, ATTN_WIDTH, tm), bf16),
        jax.ShapeDtypeStruct((bx, lx, KV_WIDTH), bf16),
        jax.ShapeDtypeStruct((bx, KV_WIDTH, lx), bf16),
        jax.ShapeDtypeStruct((bx, NA_WIDTH, lx), bf16),
        jax.ShapeDtypeStruct((bx, NA_HEADS // 2, lx, 128), bf16),
        jax.ShapeDtypeStruct((bx, NA_WIDTH, lx), bf16),
    )
    return pl.pallas_call(
        _proj_kernel,
        grid=grid,
        in_specs=[
            pl.BlockSpec((1, tm, D_MODEL), lambda b, j: (b, j, 0)),
            pl.BlockSpec((1, D_MODEL), lambda b, j: (0, 0)),
            pl.BlockSpec((P1_END, D_MODEL), lambda b, j: (0, 0)),
            pl.BlockSpec((HEAD_DIM, 1), lambda b, j: (0, 0)),
            pl.BlockSpec((HEAD_DIM, 1), lambda b, j: (0, 0)),
            pl.BlockSpec((HEAD_DIM, tm), lambda b, j: (0, j)),
            pl.BlockSpec((HEAD_DIM, tm), lambda b, j: (0, j)),
        ],
        out_specs=(
            pl.BlockSpec((1, 1, ATTN_WIDTH, tm), lambda b, j: (b, j, 0, 0)),
            pl.BlockSpec((1, tm, KV_WIDTH), lambda b, j: (b, j, 0)),
            pl.BlockSpec((1, KV_WIDTH, tm), lambda b, j: (b, 0, j)),
            pl.BlockSpec((1, NA_WIDTH, tm), lambda b, j: (b, 0, j)),
            pl.BlockSpec((1, NA_HEADS // 2, tm, 128), lambda b, j: (b, 0, j, 0)),
            pl.BlockSpec((1, NA_WIDTH, tm), lambda b, j: (b, 0, j)),
        ),
        out_shape=out_shape,
        compiler_params=_params(("parallel", "parallel")),
        name="qkv_proj",
    )(x3, norm_g, w1t, qg, kg, cos_t, sin_t)


def _colmax8(x):
    return jnp.max(x.reshape(x.shape[0] // 8, 8, x.shape[1]), axis=0)


def _colsum8(x):
    return jnp.sum(x.reshape(x.shape[0] // 8, 8, x.shape[1]), axis=0)


def _gqa_kernel(q_ref, k_ref, km_ref, vt_ref, vtm_ref, o_ref,
                s0_ref, s1_ref, m0_ref, m1_ref, vx_ref, *, tq, n_qblk):
    g = pl.program_id(1)
    row = lax.broadcasted_iota(jnp.int32, (2 * HEAD_DIM, tq), 0)
    sel = (row >= HEAD_DIM) == (g == 1)
    meta_valid = row < N_META
    n_items = n_qblk * ATTN_GROUP
    n_chunks = SEQ // GQA_TK
    blocks = [(0, META_PAD)] + [(META_PAD + c * GQA_TK, GQA_TK) for c in range(n_chunks)]

    vx_ref[0:HEAD_DIM, 0:META_PAD] = vtm_ref[0]
    vx_ref[0:HEAD_DIM, META_PAD:] = vt_ref[0]
    vx_ref[HEAD_DIM:, :] = jnp.ones((GQA_ONES_ROWS, META_PAD + SEQ), bf16)

    def item_index(t):
        return t // ATTN_GROUP, pl.multiple_of((t % ATTN_GROUP) * HEAD_DIM, HEAD_DIM)

    def load_qext(t):
        qi, off = item_index(t)
        q = q_ref[0, qi, pl.ds(off, HEAD_DIM), :]
        return jnp.where(sel, jnp.concatenate([q, q], axis=0), jnp.zeros((), bf16))

    def score_block(qext, lo, n):
        if lo == 0:
            return jnp.where(meta_valid, _nn(km_ref[0], qext), NEG)
        return _nn(k_ref[0, lo - META_PAD:lo - META_PAD + n, :], qext)

    def pv_block(s_ref, m, lo, n):
        return _nn(vx_ref[:, lo:lo + n], jnp.exp2(s_ref[lo:lo + n, :] - m).astype(bf16))

    def first_pass(qext, s_ref, m_ref, interleave):
        m8 = None
        for lo, n in blocks:
            s = score_block(qext, lo, n)
            s_ref[lo:lo + n, :] = s
            m8 = _colmax8(s) if m8 is None else jnp.maximum(m8, _colmax8(s))
            interleave(lo, n)
        m_ref[...] = jnp.max(m8, axis=0, keepdims=True)

    def fused(t_cur, s_cur, m_cur, t_nxt, s_nxt, m_nxt):
        m = m_cur[...]
        acc = []

        def second_pass(lo, n):
            d = pv_block(s_cur, m, lo, n)
            acc[:] = [acc[0] + d] if acc else [d]

        first_pass(load_qext(t_nxt), s_nxt, m_nxt, second_pass)
        qi, off = item_index(t_cur)
        o_ref[0, qi, pl.ds(off, HEAD_DIM), :] = (
            acc[0][0:HEAD_DIM] / acc[0][HEAD_DIM:HEAD_DIM + 1]).astype(bf16)

    first_pass(load_qext(0), s0_ref, m0_ref, lambda lo, n: None)

    def step(t, carry):
        t_nxt = jnp.minimum(t + 1, n_items - 1)

        @pl.when(t % 2 == 0)
        def _():
            fused(t, s0_ref, m0_ref, t_nxt, s1_ref, m1_ref)

        @pl.when(t % 2 == 1)
        def _():
            fused(t, s1_ref, m1_ref, t_nxt, s0_ref, m0_ref)

        return carry

    lax.fori_loop(0, n_items, step, 0)


def _gqa_call(qt, k_real, k_meta, vt_real, vt_meta, *, meta_queries):
    b = k_real.shape[0]
    gw = ATTN_GROUP * HEAD_DIM
    if meta_queries:
        per_tile = TOKEN_TILE // META_PAD
        tq, n_qblk = META_PAD, 1
        q_map = lambda bi, g: (0, bi // per_tile, g, bi % per_tile)
    else:
        tq, n_qblk = TOKEN_TILE, SEQ // TOKEN_TILE
        q_map = lambda bi, g: (bi, 0, g, 0)
    n_keys = META_PAD + SEQ
    return pl.pallas_call(
        functools.partial(_gqa_kernel, tq=tq, n_qblk=n_qblk),
        grid=(b, ATTN_KV_HEADS),
        in_specs=[
            pl.BlockSpec((1, n_qblk, gw, tq), q_map),
            pl.BlockSpec((1, SEQ, KV_WIDTH), lambda bi, g: (bi, 0, 0)),
            pl.BlockSpec((1, META_PAD, KV_WIDTH), lambda bi, g: (0, bi, 0)),
            pl.BlockSpec((1, HEAD_DIM, SEQ), lambda bi, g: (bi, g, 0)),
            pl.BlockSpec((1, HEAD_DIM, META_PAD), lambda bi, g: (0, g, bi)),
        ],
        out_specs=pl.BlockSpec((1, n_qblk, gw, tq), q_map),
        out_shape=jax.ShapeDtypeStruct(qt.shape, bf16),
        scratch_shapes=[
            pltpu.VMEM((n_keys, tq), f32),
            pltpu.VMEM((n_keys, tq), f32),
            pltpu.VMEM((1, tq), f32),
            pltpu.VMEM((1, tq), f32),
            pltpu.VMEM((HEAD_DIM + GQA_ONES_ROWS, n_keys), bf16),
        ],
        compiler_params=_params(("parallel", "parallel")),
        name="gqa_meta" if meta_queries else "gqa_attn",
    )(qt, k_real, k_meta, vt_real, vt_meta)


def _na_block_geometry(blk):
    r0 = blk * NA_QROWS
    kb = min(max(r0 - NA_KH // 2, 0), GRID_ROWS - NA_BAND_ROWS)
    typ = 0 if blk == 0 else (2 if blk == NA_NBLK - 1 else 1)
    return r0, kb, typ


def _na_row_window(r):
    rs = min(max(r - NA_KH // 2, 0), GRID_ROWS - NA_KH)
    return rs, rs + NA_KH


def _na_build_tables(rpb_ref, h, t_ref, tab_ref):
    kc = lax.broadcasted_iota(jnp.int32, (GRID_W, 128), 0)
    lane = lax.broadcasted_iota(jnp.int32, (GRID_W, 128), 1)
    c = lane % GRID_W
    rel = kc - c + (NA_KW - 1)
    cs = jnp.clip(c - NA_KW // 2, 0, GRID_W - NA_KW)
    cvalid = (kc >= cs) & (kc < cs + NA_KW)
    neg_tile = jnp.full((GRID_W, 128), NEG, f32)

    def build_a(a, carry):
        acc = neg_tile
        for r in range(2 * NA_KW - 1):
            acc = jnp.where(rel == r, rpb_ref[h, a, r] * LOG2E, acc)
        t_ref[a] = jnp.where(cvalid, acc, NEG)
        return carry

    lax.fori_loop(0, 2 * NA_KH - 1, build_a, 0)
    for typ, blk in enumerate((0, 1, NA_NBLK - 1)):
        r0, kb, _ = _na_block_geometry(blk)
        for i in range(NA_BAND_ROWS):
            kr = kb + i
            for jj in range(NA_QROWS // 2):
                halves = []
                for r in (r0 + 2 * jj, r0 + 2 * jj + 1):
                    lo, hi = _na_row_window(r)
                    halves.append(t_ref[kr - r + NA_KH - 1] if lo <= kr < hi else neg_tile)
                tab_ref[typ, i * GRID_W:(i + 1) * GRID_W, jj * 128:(jj + 1) * 128] = jnp.where(
                    lane < GRID_W, halves[0], halves[1])


def _na_kernel(rpb_ref, q_ref, k_ref, km_ref, vt_ref, vtm_ref, o_ref,
               t_ref, tab_ref, s0_ref, s1_ref, m0_ref, m1_ref, vx_ref):
    @pl.when(pl.program_id(1) == 0)
    def _():
        _na_build_tables(rpb_ref, pl.program_id(0), t_ref, tab_ref)

    par = pl.program_id(0) % 2
    row = lax.broadcasted_iota(jnp.int32, (2 * HEAD_DIM, NA_QBLK), 0)
    sel = (row >= HEAD_DIM) == (par == 1)
    meta_valid = row < N_META

    vx_ref[0:HEAD_DIM, 0:META_PAD] = vtm_ref[0]
    vx_ref[0:HEAD_DIM, META_PAD:] = vt_ref[0]
    vx_ref[HEAD_DIM:, :] = jnp.ones((GQA_ONES_ROWS, META_PAD + SEQ), bf16)

    def band_start(blk):
        kb = jnp.clip(blk * NA_QROWS - NA_KH // 2, 0, GRID_ROWS - NA_BAND_ROWS)
        return pl.multiple_of(kb * GRID_W, 2 * GRID_W)

    def first_pass(blk, s_ref, m_ref):
        typ = jnp.where(blk == 0, 0, jnp.where(blk == NA_NBLK - 1, 2, 1))
        q = q_ref[0, :, pl.ds(pl.multiple_of(blk * NA_QBLK, NA_QBLK), NA_QBLK)]
        qext = jnp.where(sel, jnp.concatenate([q, q], axis=0), jnp.zeros((), bf16))
        sm = jnp.where(meta_valid, _nn(km_ref[0, 0], qext), NEG)
        s = _nn(k_ref[0, 0, pl.ds(band_start(blk), NA_BAND), :], qext) + tab_ref[typ]
        s_ref[0:META_PAD, :] = sm
        s_ref[META_PAD:, :] = s
        m_ref[...] = jnp.max(jnp.maximum(_colmax8(sm), _colmax8(s)), axis=0, keepdims=True)

    def second_pass(blk, s_ref, m_ref):
        m = m_ref[...]
        pm = jnp.exp2(s_ref[0:META_PAD, :] - m).astype(bf16)
        p = jnp.exp2(s_ref[META_PAD:, :] - m).astype(bf16)
        ks = pl.multiple_of(META_PAD + band_start(blk), META_PAD)
        acc = _nn(vx_ref[:, 0:META_PAD], pm) + _nn(vx_ref[:, pl.ds(ks, NA_BAND)], p)
        o_ref[0, :, pl.ds(pl.multiple_of(blk * NA_QBLK, NA_QBLK), NA_QBLK)] = (
            acc[0:HEAD_DIM] / acc[HEAD_DIM:HEAD_DIM + 1]).astype(bf16)

    def first_pass_stage(u, s_ref, m_ref):
        for j in range(NA_STAGE_BLKS):
            first_pass(u * NA_STAGE_BLKS + j, s_ref.at[j], m_ref.at[j])

    def second_pass_stage(u, s_ref, m_ref):
        for j in range(NA_STAGE_BLKS):
            second_pass(u * NA_STAGE_BLKS + j, s_ref.at[j], m_ref.at[j])

    n_stages = NA_NBLK // NA_STAGE_BLKS
    first_pass_stage(0, s0_ref, m0_ref)

    def step(u, carry):
        nxt = jnp.minimum(u + 1, n_stages - 1)

        @pl.when(u % 2 == 0)
        def _():
            first_pass_stage(nxt, s1_ref, m1_ref)
            second_pass_stage(u, s0_ref, m0_ref)

        @pl.when(u % 2 == 1)
        def _():
            first_pass_stage(nxt, s0_ref, m0_ref)
            second_pass_stage(u, s1_ref, m1_ref)

        return carry

    lax.fori_loop(0, n_stages, step, 0)


def _na_call(qt, k_real, k_meta, vt_real, vt_meta, rpb):
    b = qt.shape[0]
    return pl.pallas_call(
        _na_kernel,
        grid=(NA_HEADS, b),
        in_specs=[
            pl.BlockSpec(memory_space=pltpu.SMEM),
            pl.BlockSpec((1, HEAD_DIM, SEQ), lambda h, bi: (bi, h, 0)),
            pl.BlockSpec((1, 1, SEQ, 128), lambda h, bi: (bi, h // 2, 0, 0)),
            pl.BlockSpec((1, 1, META_PAD, 128), lambda h, bi: (0, h // 2, bi, 0)),
            pl.BlockSpec((1, HEAD_DIM, SEQ), lambda h, bi: (bi, h, 0)),
            pl.BlockSpec((1, HEAD_DIM, META_PAD), lambda h, bi: (0, h, bi)),
        ],
        out_specs=pl.BlockSpec((1, HEAD_DIM, SEQ), lambda h, bi: (bi, h, 0)),
        out_shape=jax.ShapeDtypeStruct(qt.shape, bf16),
        scratch_shapes=[
            pltpu.VMEM((2 * NA_KH - 1, GRID_W, 128), f32),
            pltpu.VMEM((3, NA_BAND, NA_QBLK), f32),
            pltpu.VMEM((NA_STAGE_BLKS, META_PAD + NA_BAND, NA_QBLK), f32),
            pltpu.VMEM((NA_STAGE_BLKS, META_PAD + NA_BAND, NA_QBLK), f32),
            pltpu.VMEM((NA_STAGE_BLKS, 1, NA_QBLK), f32),
            pltpu.VMEM((NA_STAGE_BLKS, 1, NA_QBLK), f32),
            pltpu.VMEM((HEAD_DIM + GQA_ONES_ROWS, META_PAD + SEQ), bf16),
        ],
        compiler_params=_params(("arbitrary", "arbitrary")),
        name="na_attn",
    )(rpb, qt, k_real, k_meta, vt_real, vt_meta)


def _na_meta_kernel(q_ref, km_ref, vtm_ref, o_ref):
    row = lax.broadcasted_iota(jnp.int32, (2 * HEAD_DIM, META_PAD), 0)
    meta_valid = row < N_META
    for h in range(NA_HEADS):
        lo = h * HEAD_DIM
        q = q_ref[0, lo:lo + HEAD_DIM, :]
        sel = (row >= HEAD_DIM) == (h % 2 == 1)
        qext = jnp.where(sel, jnp.concatenate([q, q], axis=0), jnp.zeros((), bf16))
        s = jnp.where(meta_valid, _nn(km_ref[0, h // 2], qext), NEG)
        m = jnp.max(s, axis=0, keepdims=True)
        p = jnp.exp2(s - m)
        l = jnp.sum(p, axis=0, keepdims=True)
        acc = _nn(vtm_ref[0, lo:lo + HEAD_DIM, :], p.astype(bf16))
        o_ref[0, lo:lo + HEAD_DIM, :] = (acc / l).astype(bf16)


def _na_meta_call(qt_meta, k_meta, vt_meta, b):
    return pl.pallas_call(
        _na_meta_kernel,
        grid=(b,),
        in_specs=[
            pl.BlockSpec((1, NA_WIDTH, META_PAD), lambda bi: (0, 0, bi)),
            pl.BlockSpec((1, NA_HEADS // 2, META_PAD, 128), lambda bi: (0, 0, bi, 0)),
            pl.BlockSpec((1, NA_WIDTH, META_PAD), lambda bi: (0, 0, bi)),
        ],
        out_specs=pl.BlockSpec((1, NA_WIDTH, META_PAD), lambda bi: (0, 0, bi)),
        out_shape=jax.ShapeDtypeStruct(qt_meta.shape, bf16),
        compiler_params=_params(("parallel",)),
        name="na_meta",
    )(qt_meta, k_meta, vt_meta)


def _out_kernel(x_ref, g_ref, ata_ref, atb_ref, w2_ref, woa_ref, wob_ref, wout_ref, fg_ref,
                o_ref, *, final):
    x = x_ref[0]
    h = _rms_rows(x, g_ref[...]).astype(bf16)

    def gated(at_ref, lo, hi):
        z = _nt(w2_ref[lo:hi, :], h)
        return (at_ref[...].reshape(z.shape).astype(f32) * (z * _sigmoid(z))).astype(bf16)

    ya = _nn(woa_ref[...], gated(ata_ref, P2_ZA, P2_ZB))
    yb = _nn(wob_ref[...], gated(atb_ref, P2_ZB, P2_GA))
    ga = _nt(w2_ref[P2_GA:P2_GB, :], h)
    gb = _nt(w2_ref[P2_GB:P2_END, :], h)
    mix = (_sigmoid(ga) * ya + _sigmoid(gb) * yb).astype(bf16)
    out = x + _tn(mix, wout_ref[...])
    if final:
        out = _rms_rows(out, fg_ref[...])
    o_ref[0] = out


def _out_call(x3, norm_g, at_a, at_b, w2t, woat, wobt, wout, final_g, tm, final):
    bx, lx, _ = x3.shape
    const = lambda b, j: (0, 0)
    return pl.pallas_call(
        functools.partial(_out_kernel, final=final),
        grid=(bx, lx // tm),
        in_specs=[
            pl.BlockSpec((1, tm, D_MODEL), lambda b, j: (b, j, 0)),
            pl.BlockSpec((1, D_MODEL), const),
            pl.BlockSpec((1, 1, ATTN_WIDTH, tm), lambda b, j: (b, j, 0, 0)),
            pl.BlockSpec((1, NA_WIDTH, tm), lambda b, j: (b, 0, j)),
            pl.BlockSpec((P2_END, D_MODEL), const),
            pl.BlockSpec((D_MODEL, ATTN_WIDTH), const),
            pl.BlockSpec((D_MODEL, NA_WIDTH), const),
            pl.BlockSpec((D_MODEL, D_MODEL), const),
            pl.BlockSpec((1, D_MODEL), const),
        ],
        out_specs=pl.BlockSpec((1, tm, D_MODEL), lambda b, j: (b, j, 0)),
        out_shape=jax.ShapeDtypeStruct(x3.shape, f32),
        compiler_params=_params(("parallel", "parallel")),
        name="gate_out_proj",
    )(x3, norm_g, at_a, at_b, w2t, woat, wobt, wout, final_g)


def _rope_tables():
    t = np.arange(SEQ)
    axis_dim = HEAD_DIM // 2
    inv = jnp.asarray(ROPE_THETA, f32) ** (-jnp.arange(0, axis_dim, 2, dtype=f32) / axis_dim)
    ang_r = jnp.asarray(t // GRID_W, f32)[None, :] * inv[:, None]
    ang_c = jnp.asarray(t % GRID_W, f32)[None, :] * inv[:, None]
    ang = jnp.concatenate([ang_r, ang_r, ang_c, ang_c], axis=0)
    sign = np.repeat(np.array([-1.0, 1.0, -1.0, 1.0], np.float32), axis_dim // 2)[:, None]
    return jnp.cos(ang), jnp.sin(ang) * sign


def kernel(x, meta_tokens, norm_g, w_in, q_norm_g, k_norm_g, na_rpb, w_o_attn, w_o_na, w_out,
           final_norm_g):
    b = x.shape[0]
    depth = w_in.shape[0]
    cos_r, sin_r = _rope_tables()
    cos_m = jnp.ones((HEAD_DIM, b * META_PAD), f32)
    sin_m = jnp.zeros((HEAD_DIM, b * META_PAD), f32)

    meta_pad = jnp.concatenate(
        [meta_tokens.astype(f32), jnp.zeros((META_PAD - N_META, D_MODEL), f32)], axis=0)
    xm = jnp.tile(meta_pad, (b, 1))[None]
    xr = x
    fg = final_norm_g.reshape(1, D_MODEL)

    for i in range(depth):
        wi = w_in[i]
        qa, ka, va, za, qb, kb, vb, zb, ga, gb = (
            wi[:, 0:512], wi[:, 512:640], wi[:, 640:768], wi[:, 768:1280], wi[:, 1280:1792],
            wi[:, 1792:2304], wi[:, 2304:2816], wi[:, 2816:3328], wi[:, 3328:4352], wi[:, 4352:5376])
        w1t = jnp.concatenate([qa, ka, va, qb, kb, vb], axis=1).T.astype(bf16)
        w2t = jnp.concatenate([za, zb, ga, gb], axis=1).T.astype(bf16)
        woat = w_o_attn[i].T.astype(bf16)
        wobt = w_o_na[i].T.astype(bf16)
        wout = w_out[i].astype(bf16)
        ng = norm_g[i].reshape(1, D_MODEL)
        qg = q_norm_g[i].reshape(HEAD_DIM, 1)
        kg = k_norm_g[i].reshape(HEAD_DIM, 1)
        last = i == depth - 1

        qta, k_a, vta, qtb, k_b, vtb = _proj_call(xr, ng, w1t, qg, kg, cos_r, sin_r, TOKEN_TILE)
        qta_m, k_a_m, vta_m, qtb_m, k_b_m, vtb_m = _proj_call(
            xm, ng, w1t, qg, kg, cos_m, sin_m, TOKEN_TILE)

        at_a = _gqa_call(qta, k_a, k_a_m, vta, vta_m, meta_queries=False)
        at_b = _na_call(qtb, k_b, k_b_m, vtb, vtb_m, na_rpb[i])
        xr_new = _out_call(xr, ng, at_a, at_b, w2t, woat, wobt, wout, fg, TOKEN_TILE, last)
        if not last:
            at_a_m = _gqa_call(qta_m, k_a, k_a_m, vta, vta_m, meta_queries=True)
            at_b_m = _na_meta_call(qtb_m, k_b_m, vtb_m, b)
            xm = _out_call(xm, ng, at_a_m, at_b_m, w2t, woat, wobt, wout, fg, TOKEN_TILE, False)
        xr = xr_new
    return xr
```

```python
import functools

import numpy as np
import jax
import jax.numpy as jnp
from jax import lax
from jax.experimental import pallas as pl
from jax.experimental.pallas import tpu as pltpu

D_MODEL = 1024
SEQ = 4096
GRID_W = 64
GRID_ROWS = SEQ // GRID_W
N_META = 16
META_PAD = 128
HEAD_DIM = 64
ATTN_HEADS = 8
ATTN_KV_HEADS = 2
ATTN_GROUP = ATTN_HEADS // ATTN_KV_HEADS
NA_HEADS = 8
NA_KH = 8
NA_KW = 16
ROPE_THETA = 10000.0
EPS = 1e-6
SCALE = HEAD_DIM ** -0.5
LOG2E = 1.4426950408889634
Q_SCALE = SCALE * LOG2E
NEG = -1e30

ATTN_WIDTH = ATTN_HEADS * HEAD_DIM
KV_WIDTH = ATTN_KV_HEADS * HEAD_DIM
NA_WIDTH = NA_HEADS * HEAD_DIM
KN_ROWS = 8

P1_QA, P1_KA, P1_VA, P1_QB, P1_KB, P1_VB, P1_END = 0, 512, 640, 768, 1280, 1792, 2304
P2_ZA, P2_ZB, P2_GA, P2_GB, P2_END = 0, 512, 1024, 2048, 3072

NA_QROWS = 4
NA_QBLK = NA_QROWS * GRID_W
NA_BAND_ROWS = 12
NA_BAND = NA_BAND_ROWS * GRID_W
NA_NBLK = GRID_ROWS // NA_QROWS
NA_UNROLL = 4

TOKEN_TILE = 512
GQA_TK = 512
GQA_ONES_ROWS = 16
BOUND_MARGIN = 1.0 + 2.0 ** -7
DENOM_FLOOR = 2.0 ** -64
VMEM_LIMIT = 56 * 1024 * 1024

bf16 = jnp.bfloat16
f32 = jnp.float32


def _nn(a, b):
    return lax.dot_general(a, b, (((1,), (0,)), ((), ())), preferred_element_type=f32)


def _nt(a, b):
    return lax.dot_general(a, b, (((1,), (1,)), ((), ())), preferred_element_type=f32)


def _tn(a, b):
    return lax.dot_general(a, b, (((0,), (0,)), ((), ())), preferred_element_type=f32)


def _rms_rows(x, g):
    return x * lax.rsqrt(jnp.mean(x * x, axis=-1, keepdims=True) + EPS) * g


def _sigmoid(x):
    return 1.0 / (1.0 + jnp.exp(-x))


def _params(sem, flags=None):
    return pltpu.CompilerParams(dimension_semantics=sem, vmem_limit_bytes=VMEM_LIMIT, flags=flags)


def _proj_kernel(x_ref, g_ref, w_ref, qg_ref, kg_ref, cos_ref, sin_ref,
                 qta_ref, ka_ref, vta_ref, qtb_ref, kb_ref, vtb_ref, kna_ref, knb_ref):
    h = _rms_rows(x_ref[0], g_ref[...]).astype(bf16)
    tm = h.shape[0]

    def head_sq_norms(kt):
        sq = (kt * kt).reshape(kt.shape[0] // HEAD_DIM, HEAD_DIM, tm)
        return jnp.sum(sq, axis=1)
    cos = cos_ref[...]
    sin = sin_ref[...]

    def proj(lo, hi):
        return _nt(w_ref[lo:hi, :], h)

    def norm_rope(t, gain):
        y = t * lax.rsqrt(jnp.mean(t * t, axis=0, keepdims=True) + EPS) * gain
        sw = jnp.concatenate([y[16:32], y[0:16], y[48:64], y[32:48]], axis=0)
        return y * cos + sw * sin

    qg = qg_ref[...]
    kg = kg_ref[...]
    qa = proj(P1_QA, P1_KA)
    for hd in range(ATTN_HEADS):
        lo = hd * HEAD_DIM
        qta_ref[0, 0, lo:lo + HEAD_DIM, :] = (norm_rope(qa[lo:lo + HEAD_DIM], qg) * Q_SCALE).astype(bf16)
    ka = proj(P1_KA, P1_VA)
    kt = jnp.concatenate([norm_rope(ka[0:HEAD_DIM], kg), norm_rope(ka[HEAD_DIM:], kg)], axis=0)
    ka_ref[0] = kt.T.astype(bf16)
    kna_ref[0] = jnp.concatenate(
        [head_sq_norms(kt), jnp.zeros((KN_ROWS - ATTN_KV_HEADS, tm), f32)], axis=0)
    vta_ref[0] = proj(P1_VA, P1_QB).astype(bf16)
    qtb_ref[0] = (proj(P1_QB, P1_KB) * Q_SCALE).astype(bf16)
    kbt = proj(P1_KB, P1_VB)
    for p in range(NA_HEADS // 2):
        kb_ref[0, p] = kbt[128 * p:128 * (p + 1)].T.astype(bf16)
    knb_ref[0] = head_sq_norms(kbt)
    vtb_ref[0] = proj(P1_VB, P1_END).astype(bf16)


def _proj_call(x3, norm_g, w1t, qg, kg, cos_t, sin_t, tm):
    bx, lx, _ = x3.shape
    grid = (bx, lx // tm)
    out_shape = (
        jax.ShapeDtypeStruct((bx, lx // tm, ATTN_WIDTH, tm), bf16),
        jax.ShapeDtypeStruct((bx, lx, KV_WIDTH), bf16),
        jax.ShapeDtypeStruct((bx, KV_WIDTH, lx), bf16),
        jax.ShapeDtypeStruct((bx, NA_WIDTH, lx), bf16),
        jax.ShapeDtypeStruct((bx, NA_HEADS // 2, lx, 128), bf16),
        jax.ShapeDtypeStruct((bx, NA_WIDTH, lx), bf16),
        jax.ShapeDtypeStruct((bx, KN_ROWS, lx), f32),
        jax.ShapeDtypeStruct((bx, KN_ROWS, lx), f32),
    )
    return pl.pallas_call(
        _proj_kernel,
        grid=grid,
        in_specs=[
            pl.BlockSpec((1, tm, D_MODEL), lambda b, j: (b, j, 0)),
            pl.BlockSpec((1, D_MODEL), lambda b, j: (0, 0)),
            pl.BlockSpec((P1_END, D_MODEL), lambda b, j: (0, 0)),
            pl.BlockSpec((HEAD_DIM, 1), lambda b, j: (0, 0)),
            pl.BlockSpec((HEAD_DIM, 1), lambda b, j: (0, 0)),
            pl.BlockSpec((HEAD_DIM, tm), lambda b, j: (0, j)),
            pl.BlockSpec((HEAD_DIM, tm), lambda b, j: (0, j)),
        ],
        out_specs=(
            pl.BlockSpec((1, 1, ATTN_WIDTH, tm), lambda b, j: (b, j, 0, 0)),
            pl.BlockSpec((1, tm, KV_WIDTH), lambda b, j: (b, j, 0)),
            pl.BlockSpec((1, KV_WIDTH, tm), lambda b, j: (b, 0, j)),
            pl.BlockSpec((1, NA_WIDTH, tm), lambda b, j: (b, 0, j)),
            pl.BlockSpec((1, NA_HEADS // 2, tm, 128), lambda b, j: (b, 0, j, 0)),
            pl.BlockSpec((1, NA_WIDTH, tm), lambda b, j: (b, 0, j)),
            pl.BlockSpec((1, KN_ROWS, tm), lambda b, j: (b, 0, j)),
            pl.BlockSpec((1, KN_ROWS, tm), lambda b, j: (b, 0, j)),
        ),
        out_shape=out_shape,
        compiler_params=_params(("parallel", "parallel")),
        name="qkv_proj",
    )(x3, norm_g, w1t, qg, kg, cos_t, sin_t)


def _colmax8(x):
    return jnp.max(x.reshape(x.shape[0] // 8, 8, x.shape[1]), axis=0)


def _colsum8(x):
    return jnp.sum(x.reshape(x.shape[0] // 8, 8, x.shape[1]), axis=0)


def _max_sq_key_norm(kn, knm):
    lane = lax.broadcasted_iota(jnp.int32, knm.shape, 1)
    return jnp.maximum(jnp.max(kn, axis=1, keepdims=True),
                       jnp.max(jnp.where(lane < N_META, knm, 0.0), axis=1, keepdims=True))


def _gqa_kernel(q_ref, k_ref, km_ref, vt_ref, vtm_ref, kn_ref, knm_ref, o_ref, s_ref, vx_ref,
                *, tq, n_qblk):
    g = pl.program_id(1)
    row = lax.broadcasted_iota(jnp.int32, (2 * HEAD_DIM, tq), 0)
    sel = (row >= HEAD_DIM) == (g == 1)
    meta_valid = row < N_META
    n_items = n_qblk * ATTN_GROUP
    n_chunks = SEQ // GQA_TK
    blocks = [(0, META_PAD)] + [(META_PAD + c * GQA_TK, GQA_TK) for c in range(n_chunks)]

    vx_ref[0:HEAD_DIM, 0:META_PAD] = vtm_ref[0]
    vx_ref[0:HEAD_DIM, META_PAD:] = vt_ref[0]
    vx_ref[HEAD_DIM:, :] = jnp.ones((GQA_ONES_ROWS, META_PAD + SEQ), bf16)

    def item_index(t):
        return t // ATTN_GROUP, pl.multiple_of((t % ATTN_GROUP) * HEAD_DIM, HEAD_DIM)

    def load_qext(t):
        qi, off = item_index(t)
        q = q_ref[0, qi, pl.ds(off, HEAD_DIM), :]
        return jnp.where(sel, jnp.concatenate([q, q], axis=0), jnp.zeros((), bf16))

    def score_block(qext, lo, n):
        if lo == 0:
            return jnp.where(meta_valid, _nn(km_ref[0], qext), NEG)
        return _nn(k_ref[0, lo - META_PAD:lo - META_PAD + n, :], qext)

    def store_item(t, acc):
        l = acc[HEAD_DIM:HEAD_DIM + 1]
        qi, off = item_index(t)
        o_ref[0, qi, pl.ds(off, HEAD_DIM), :] = (acc[0:HEAD_DIM] / l).astype(bf16)
        return l

    kmax2 = _max_sq_key_norm(kn_ref[0, pl.ds(g, 1), :], knm_ref[0, pl.ds(g, 1), :])

    def fast_item(t, l_min):
        qext = load_qext(t)
        qf = qext.astype(f32)
        bound = jnp.sqrt(jnp.sum(qf * qf, axis=0, keepdims=True) * kmax2) * BOUND_MARGIN
        acc, pending = None, None
        for lo, n in blocks + [(None, None)]:
            s = score_block(qext, lo, n) if lo is not None else None
            if pending is not None:
                s_prev, plo, pn = pending
                d = _nn(vx_ref[:, plo:plo + pn], jnp.exp2(s_prev - bound).astype(bf16))
                acc = d if acc is None else acc + d
            pending = (s, lo, n)
        return jnp.minimum(l_min, store_item(t, acc))

    l_min = lax.fori_loop(0, n_items, fast_item, jnp.full((1, tq), 1.0, f32), unroll=4)

    def exact_item(t, carry):
        qext = load_qext(t)
        m8 = None
        for lo, n in blocks:
            s = score_block(qext, lo, n)
            s_ref[lo:lo + n, :] = s
            m8 = _colmax8(s) if m8 is None else jnp.maximum(m8, _colmax8(s))
        m = jnp.max(m8, axis=0, keepdims=True)
        acc = None
        for lo, n in blocks:
            d = _nn(vx_ref[:, lo:lo + n], jnp.exp2(s_ref[lo:lo + n, :] - m).astype(bf16))
            acc = d if acc is None else acc + d
        store_item(t, acc)
        return carry

    @pl.when(jnp.min(l_min) < DENOM_FLOOR)
    def _():
        lax.fori_loop(0, n_items, exact_item, 0)


def _gqa_call(qt, k_real, k_meta, vt_real, vt_meta, kn_real, kn_meta, *, meta_queries):
    b = k_real.shape[0]
    gw = ATTN_GROUP * HEAD_DIM
    if meta_queries:
        per_tile = TOKEN_TILE // META_PAD
        tq, n_qblk = META_PAD, 1
        q_map = lambda bi, g: (0, bi // per_tile, g, bi % per_tile)
    else:
        tq, n_qblk = TOKEN_TILE, SEQ // TOKEN_TILE
        q_map = lambda bi, g: (bi, 0, g, 0)
    n_keys = META_PAD + SEQ
    return pl.pallas_call(
        functools.partial(_gqa_kernel, tq=tq, n_qblk=n_qblk),
        grid=(b, ATTN_KV_HEADS),
        in_specs=[
            pl.BlockSpec((1, n_qblk, gw, tq), q_map),
            pl.BlockSpec((1, SEQ, KV_WIDTH), lambda bi, g: (bi, 0, 0)),
            pl.BlockSpec((1, META_PAD, KV_WIDTH), lambda bi, g: (0, bi, 0)),
            pl.BlockSpec((1, HEAD_DIM, SEQ), lambda bi, g: (bi, g, 0)),
            pl.BlockSpec((1, HEAD_DIM, META_PAD), lambda bi, g: (0, g, bi)),
            pl.BlockSpec((1, KN_ROWS, SEQ), lambda bi, g: (bi, 0, 0)),
            pl.BlockSpec((1, KN_ROWS, META_PAD), lambda bi, g: (0, 0, bi)),
        ],
        out_specs=pl.BlockSpec((1, n_qblk, gw, tq), q_map),
        out_shape=jax.ShapeDtypeStruct(qt.shape, bf16),
        scratch_shapes=[
            pltpu.VMEM((n_keys, tq), f32),
            pltpu.VMEM((HEAD_DIM + GQA_ONES_ROWS, n_keys), bf16),
        ],
        compiler_params=_params(("parallel", "parallel")),
        name="gqa_meta" if meta_queries else "gqa_attn",
    )(qt, k_real, k_meta, vt_real, vt_meta, kn_real, kn_meta)


def _na_block_geometry(blk):
    r0 = blk * NA_QROWS
    kb = min(max(r0 - NA_KH // 2, 0), GRID_ROWS - NA_BAND_ROWS)
    typ = 0 if blk == 0 else (2 if blk == NA_NBLK - 1 else 1)
    return r0, kb, typ


def _na_row_window(r):
    rs = min(max(r - NA_KH // 2, 0), GRID_ROWS - NA_KH)
    return rs, rs + NA_KH


def _na_build_tables(rpb_ref, h, t_ref, tab_ref):
    kc = lax.broadcasted_iota(jnp.int32, (GRID_W, 128), 0)
    lane = lax.broadcasted_iota(jnp.int32, (GRID_W, 128), 1)
    c = lane % GRID_W
    rel = kc - c + (NA_KW - 1)
    cs = jnp.clip(c - NA_KW // 2, 0, GRID_W - NA_KW)
    cvalid = (kc >= cs) & (kc < cs + NA_KW)
    neg_tile = jnp.full((GRID_W, 128), NEG, f32)

    def build_a(a, carry):
        acc = neg_tile
        for r in range(2 * NA_KW - 1):
            acc = jnp.where(rel == r, rpb_ref[h, a, r] * LOG2E, acc)
        t_ref[a] = jnp.where(cvalid, acc, NEG)
        return carry

    lax.fori_loop(0, 2 * NA_KH - 1, build_a, 0)
    for typ, blk in enumerate((0, 1, NA_NBLK - 1)):
        r0, kb, _ = _na_block_geometry(blk)
        for i in range(NA_BAND_ROWS):
            kr = kb + i
            for jj in range(NA_QROWS // 2):
                halves = []
                for r in (r0 + 2 * jj, r0 + 2 * jj + 1):
                    lo, hi = _na_row_window(r)
                    halves.append(t_ref[kr - r + NA_KH - 1] if lo <= kr < hi else neg_tile)
                tab_ref[typ, i * GRID_W:(i + 1) * GRID_W, jj * 128:(jj + 1) * 128] = jnp.where(
                    lane < GRID_W, halves[0], halves[1])


def _na_kernel(rpb_ref, q_ref, k_ref, km_ref, vt_ref, vtm_ref, kn_ref, knm_ref, o_ref,
               t_ref, tab_ref, bmax_ref, s_ref, vx_ref):
    @pl.when(pl.program_id(1) == 0)
    def _():
        _na_build_tables(rpb_ref, pl.program_id(0), t_ref, tab_ref)
        bmax_ref[...] = jnp.max(jnp.max(tab_ref[...], axis=0), axis=(0, 1), keepdims=True)

    par = pl.program_id(0) % 2
    row = lax.broadcasted_iota(jnp.int32, (2 * HEAD_DIM, NA_QBLK), 0)
    sel = (row >= HEAD_DIM) == (par == 1)
    meta_valid = row < N_META

    vx_ref[0:HEAD_DIM, 0:META_PAD] = vtm_ref[0]
    vx_ref[0:HEAD_DIM, META_PAD:] = vt_ref[0]
    vx_ref[HEAD_DIM:, :] = jnp.ones((GQA_ONES_ROWS, META_PAD + SEQ), bf16)

    def band_start(blk):
        kb = jnp.clip(blk * NA_QROWS - NA_KH // 2, 0, GRID_ROWS - NA_BAND_ROWS)
        return pl.multiple_of(kb * GRID_W, 2 * GRID_W)

    def load_qext(blk):
        q = q_ref[0, :, pl.ds(pl.multiple_of(blk * NA_QBLK, NA_QBLK), NA_QBLK)]
        return jnp.where(sel, jnp.concatenate([q, q], axis=0), jnp.zeros((), bf16))

    def block_scores(blk, qext):
        typ = jnp.where(blk == 0, 0, jnp.where(blk == NA_NBLK - 1, 2, 1))
        sm = jnp.where(meta_valid, _nn(km_ref[0, 0], qext), NEG)
        s = _nn(k_ref[0, 0, pl.ds(band_start(blk), NA_BAND), :], qext) + tab_ref[typ]
        return sm, s

    def block_output(blk, pm, p):
        ks = pl.multiple_of(META_PAD + band_start(blk), META_PAD)
        acc = _nn(vx_ref[:, 0:META_PAD], pm) + _nn(vx_ref[:, pl.ds(ks, NA_BAND)], p)
        l = acc[HEAD_DIM:HEAD_DIM + 1]
        o_ref[0, :, pl.ds(pl.multiple_of(blk * NA_QBLK, NA_QBLK), NA_QBLK)] = (
            acc[0:HEAD_DIM] / l).astype(bf16)
        return l

    h = pl.program_id(0)
    kmax2 = _max_sq_key_norm(kn_ref[0, pl.ds(h, 1), :], knm_ref[0, pl.ds(h, 1), :])
    bmax = bmax_ref[...]

    def fast_stage(u, l_min):
        pending = None
        for j in range(NA_UNROLL + 1):
            if j < NA_UNROLL:
                blk = u * NA_UNROLL + j
                qext = load_qext(blk)
                qf = qext.astype(f32)
                bound = jnp.sqrt(jnp.sum(qf * qf, axis=0, keepdims=True) * kmax2) * BOUND_MARGIN + bmax
                nxt = (blk, bound) + block_scores(blk, qext)
            if pending is not None:
                pblk, pbound, sm, s = pending
                l = block_output(pblk, jnp.exp2(sm - pbound).astype(bf16),
                                 jnp.exp2(s - pbound).astype(bf16))
                l_min = jnp.minimum(l_min, l)
            pending = nxt
        return l_min

    l_min = lax.fori_loop(0, NA_NBLK // NA_UNROLL, fast_stage, jnp.full((1, NA_QBLK), 1.0, f32))

    def exact_block(blk, carry):
        sm, s = block_scores(blk, load_qext(blk))
        s_ref[0:META_PAD, :] = sm
        s_ref[META_PAD:, :] = s
        m = jnp.max(jnp.maximum(_colmax8(sm), _colmax8(s)), axis=0, keepdims=True)
        block_output(blk, jnp.exp2(s_ref[0:META_PAD, :] - m).astype(bf16),
                     jnp.exp2(s_ref[META_PAD:, :] - m).astype(bf16))
        return carry

    @pl.when(jnp.min(l_min) < DENOM_FLOOR)
    def _():
        lax.fori_loop(0, NA_NBLK, exact_block, 0)


def _na_call(qt, k_real, k_meta, vt_real, vt_meta, kn_real, kn_meta, rpb):
    b = qt.shape[0]
    return pl.pallas_call(
        _na_kernel,
        grid=(NA_HEADS, b),
        in_specs=[
            pl.BlockSpec(memory_space=pltpu.SMEM),
            pl.BlockSpec((1, HEAD_DIM, SEQ), lambda h, bi: (bi, h, 0)),
            pl.BlockSpec((1, 1, SEQ, 128), lambda h, bi: (bi, h // 2, 0, 0)),
            pl.BlockSpec((1, 1, META_PAD, 128), lambda h, bi: (0, h // 2, bi, 0)),
            pl.BlockSpec((1, HEAD_DIM, SEQ), lambda h, bi: (bi, h, 0)),
            pl.BlockSpec((1, HEAD_DIM, META_PAD), lambda h, bi: (0, h, bi)),
            pl.BlockSpec((1, KN_ROWS, SEQ), lambda h, bi: (bi, 0, 0)),
            pl.BlockSpec((1, KN_ROWS, META_PAD), lambda h, bi: (0, 0, bi)),
        ],
        out_specs=pl.BlockSpec((1, HEAD_DIM, SEQ), lambda h, bi: (bi, h, 0)),
        out_shape=jax.ShapeDtypeStruct(qt.shape, bf16),
        scratch_shapes=[
            pltpu.VMEM((2 * NA_KH - 1, GRID_W, 128), f32),
            pltpu.VMEM((3, NA_BAND, NA_QBLK), f32),
            pltpu.VMEM((1, 1), f32),
            pltpu.VMEM((META_PAD + NA_BAND, NA_QBLK), f32),
            pltpu.VMEM((HEAD_DIM + GQA_ONES_ROWS, META_PAD + SEQ), bf16),
        ],
        compiler_params=_params(("arbitrary", "arbitrary")),
        name="na_attn",
    )(rpb, qt, k_real, k_meta, vt_real, vt_meta, kn_real, kn_meta)


def _na_meta_kernel(q_ref, km_ref, vtm_ref, o_ref):
    row = lax.broadcasted_iota(jnp.int32, (2 * HEAD_DIM, META_PAD), 0)
    meta_valid = row < N_META
    for h in range(NA_HEADS):
        lo = h * HEAD_DIM
        q = q_ref[0, lo:lo + HEAD_DIM, :]
        sel = (row >= HEAD_DIM) == (h % 2 == 1)
        qext = jnp.where(sel, jnp.concatenate([q, q], axis=0), jnp.zeros((), bf16))
        s = jnp.where(meta_valid, _nn(km_ref[0, h // 2], qext), NEG)
        m = jnp.max(s, axis=0, keepdims=True)
        p = jnp.exp2(s - m)
        l = jnp.sum(p, axis=0, keepdims=True)
        acc = _nn(vtm_ref[0, lo:lo + HEAD_DIM, :], p.astype(bf16))
        o_ref[0, lo:lo + HEAD_DIM, :] = (acc / l).astype(bf16)


def _na_meta_call(qt_meta, k_meta, vt_meta, b):
    return pl.pallas_call(
        _na_meta_kernel,
        grid=(b,),
        in_specs=[
            pl.BlockSpec((1, NA_WIDTH, META_PAD), lambda bi: (0, 0, bi)),
            pl.BlockSpec((1, NA_HEADS // 2, META_PAD, 128), lambda bi: (0, 0, bi, 0)),
            pl.BlockSpec((1, NA_WIDTH, META_PAD), lambda bi: (0, 0, bi)),
        ],
        out_specs=pl.BlockSpec((1, NA_WIDTH, META_PAD), lambda bi: (0, 0, bi)),
        out_shape=jax.ShapeDtypeStruct(qt_meta.shape, bf16),
        compiler_params=_params(("parallel",)),
        name="na_meta",
    )(qt_meta, k_meta, vt_meta)


def _out_kernel(x_ref, g_ref, ata_ref, atb_ref, w2_ref, woa_ref, wob_ref, wout_ref, fg_ref,
                o_ref, *, final):
    x = x_ref[0]
    h = _rms_rows(x, g_ref[...]).astype(bf16)

    def gated(at_ref, lo, hi):
        z = _nt(w2_ref[lo:hi, :], h)
        return (at_ref[...].reshape(z.shape).astype(f32) * (z * _sigmoid(z))).astype(bf16)

    ya = _nn(woa_ref[...], gated(ata_ref, P2_ZA, P2_ZB))
    yb = _nn(wob_ref[...], gated(atb_ref, P2_ZB, P2_GA))
    ga = _nt(w2_ref[P2_GA:P2_GB, :], h)
    gb = _nt(w2_ref[P2_GB:P2_END, :], h)
    mix = (_sigmoid(ga) * ya + _sigmoid(gb) * yb).astype(bf16)
    out = x + _tn(mix, wout_ref[...])
    if final:
        out = _rms_rows(out, fg_ref[...])
    o_ref[0] = out


def _out_call(x3, norm_g, at_a, at_b, w2t, woat, wobt, wout, final_g, tm, final):
    bx, lx, _ = x3.shape
    const = lambda b, j: (0, 0)
    return pl.pallas_call(
        functools.partial(_out_kernel, final=final),
        grid=(bx, lx // tm),
        in_specs=[
            pl.BlockSpec((1, tm, D_MODEL), lambda b, j: (b, j, 0)),
            pl.BlockSpec((1, D_MODEL), const),
            pl.BlockSpec((1, 1, ATTN_WIDTH, tm), lambda b, j: (b, j, 0, 0)),
            pl.BlockSpec((1, NA_WIDTH, tm), lambda b, j: (b, 0, j)),
            pl.BlockSpec((P2_END, D_MODEL), const),
            pl.BlockSpec((D_MODEL, ATTN_WIDTH), const),
            pl.BlockSpec((D_MODEL, NA_WIDTH), const),
            pl.BlockSpec((D_MODEL, D_MODEL), const),
            pl.BlockSpec((1, D_MODEL), const),
        ],
        out_specs=pl.BlockSpec((1, tm, D_MODEL), lambda b, j: (b, j, 0)),
        out_shape=jax.ShapeDtypeStruct(x3.shape, f32),
        compiler_params=_params(("parallel", "parallel")),
        name="gate_out_proj",
    )(x3, norm_g, at_a, at_b, w2t, woat, wobt, wout, final_g)


def _rope_tables():
    t = np.arange(SEQ)
    axis_dim = HEAD_DIM // 2
    inv = jnp.asarray(ROPE_THETA, f32) ** (-jnp.arange(0, axis_dim, 2, dtype=f32) / axis_dim)
    ang_r = jnp.asarray(t // GRID_W, f32)[None, :] * inv[:, None]
    ang_c = jnp.asarray(t % GRID_W, f32)[None, :] * inv[:, None]
    ang = jnp.concatenate([ang_r, ang_r, ang_c, ang_c], axis=0)
    sign = np.repeat(np.array([-1.0, 1.0, -1.0, 1.0], np.float32), axis_dim // 2)[:, None]
    return jnp.cos(ang), jnp.sin(ang) * sign


def kernel(x, meta_tokens, norm_g, w_in, q_norm_g, k_norm_g, na_rpb, w_o_attn, w_o_na, w_out,
           final_norm_g):
    b = x.shape[0]
    depth = w_in.shape[0]
    cos_r, sin_r = _rope_tables()
    cos_m = jnp.ones((HEAD_DIM, b * META_PAD), f32)
    sin_m = jnp.zeros((HEAD_DIM, b * META_PAD), f32)

    meta_pad = jnp.concatenate(
        [meta_tokens.astype(f32), jnp.zeros((META_PAD - N_META, D_MODEL), f32)], axis=0)
    xm = jnp.tile(meta_pad, (b, 1))[None]
    xr = x
    fg = final_norm_g.reshape(1, D_MODEL)

    for i in range(depth):
        wi = w_in[i]
        qa, ka, va, za, qb, kb, vb, zb, ga, gb = (
            wi[:, 0:512], wi[:, 512:640], wi[:, 640:768], wi[:, 768:1280], wi[:, 1280:1792],
            wi[:, 1792:2304], wi[:, 2304:2816], wi[:, 2816:3328], wi[:, 3328:4352], wi[:, 4352:5376])
        w1t = jnp.concatenate([qa, ka, va, qb, kb, vb], axis=1).T.astype(bf16)
        w2t = jnp.concatenate([za, zb, ga, gb], axis=1).T.astype(bf16)
        woat = w_o_attn[i].T.astype(bf16)
        wobt = w_o_na[i].T.astype(bf16)
        wout = w_out[i].astype(bf16)
        ng = norm_g[i].reshape(1, D_MODEL)
        qg = q_norm_g[i].reshape(HEAD_DIM, 1)
        kg = k_norm_g[i].reshape(HEAD_DIM, 1)
        last = i == depth - 1

        qta, k_a, vta, qtb, k_b, vtb, kn_a, kn_b = _proj_call(
            xr, ng, w1t, qg, kg, cos_r, sin_r, TOKEN_TILE)
        qta_m, k_a_m, vta_m, qtb_m, k_b_m, vtb_m, kn_a_m, kn_b_m = _proj_call(
            xm, ng, w1t, qg, kg, cos_m, sin_m, TOKEN_TILE)

        at_a = _gqa_call(qta, k_a, k_a_m, vta, vta_m, kn_a, kn_a_m, meta_queries=False)
        at_b = _na_call(qtb, k_b, k_b_m, vtb, vtb_m, kn_b, kn_b_m, na_rpb[i])
        xr_new = _out_call(xr, ng, at_a, at_b, w2t, woat, wobt, wout, fg, TOKEN_TILE, last)
        if not last:
            at_a_m = _gqa_call(qta_m, k_a, k_a_m, vta, vta_m, kn_a, kn_a_m, meta_queries=True)
            at_b_m = _na_meta_call(qtb_m, k_b_m, vtb_m, b)
            xm = _out_call(xm, ng, at_a_m, at_b_m, w2t, woat, wobt, wout, fg, TOKEN_TILE, False)
        xr = xr_new
    return xr
```

```python
import functools

import numpy as np
import jax
import jax.numpy as jnp
from jax import lax
from jax.experimental import pallas as pl
from jax.experimental.pallas import tpu as pltpu

D_MODEL = 1024
SEQ = 4096
GRID_W = 64
GRID_ROWS = SEQ // GRID_W
N_META = 16
META_PAD = 128
HEAD_DIM = 64
ATTN_HEADS = 8
ATTN_KV_HEADS = 2
ATTN_GROUP = ATTN_HEADS // ATTN_KV_HEADS
NA_HEADS = 8
NA_KH = 8
NA_KW = 16
ROPE_THETA = 10000.0
EPS = 1e-6
SCALE = HEAD_DIM ** -0.5
LOG2E = 1.4426950408889634
Q_SCALE = SCALE * LOG2E
NEG = -1e30

ATTN_WIDTH = ATTN_HEADS * HEAD_DIM
KV_WIDTH = ATTN_KV_HEADS * HEAD_DIM
NA_WIDTH = NA_HEADS * HEAD_DIM

P1_QA, P1_KA, P1_VA, P1_QB, P1_KB, P1_VB, P1_END = 0, 512, 640, 768, 1280, 1792, 2304
P2_ZA, P2_ZB, P2_GA, P2_GB, P2_END = 0, 512, 1024, 2048, 3072

NA_QROWS = 4
NA_QBLK = NA_QROWS * GRID_W
NA_BAND_ROWS = 12
NA_BAND = NA_BAND_ROWS * GRID_W
NA_NBLK = GRID_ROWS // NA_QROWS
NA_UNROLL = 16

TOKEN_TILE = 512
GQA_TK = 512
GQA_ONES_ROWS = 16
DENOM_FLOOR = 2.0 ** -64
DENOM_CEIL = 2.0 ** 100
VMEM_LIMIT = 56 * 1024 * 1024

bf16 = jnp.bfloat16
f32 = jnp.float32


def _nn(a, b):
    return lax.dot_general(a, b, (((1,), (0,)), ((), ())), preferred_element_type=f32)


def _nt(a, b):
    return lax.dot_general(a, b, (((1,), (1,)), ((), ())), preferred_element_type=f32)


def _tn(a, b):
    return lax.dot_general(a, b, (((0,), (0,)), ((), ())), preferred_element_type=f32)


def _rms_rows(x, g):
    return x * lax.rsqrt(jnp.mean(x * x, axis=-1, keepdims=True) + EPS) * g


def _sigmoid(x):
    return 1.0 / (1.0 + jnp.exp(-x))


def _params(sem, flags=None):
    return pltpu.CompilerParams(dimension_semantics=sem, vmem_limit_bytes=VMEM_LIMIT, flags=flags)


def _proj_kernel(x_ref, g_ref, w_ref, qg_ref, kg_ref, cos_ref, sin_ref,
                 qta_ref, ka_ref, vta_ref, qtb_ref, kb_ref, vtb_ref):
    h = _rms_rows(x_ref[0], g_ref[...]).astype(bf16)
    cos = cos_ref[...]
    sin = sin_ref[...]

    def proj(lo, hi):
        return _nt(w_ref[lo:hi, :], h)

    def norm_rope(t, gain):
        y = t * lax.rsqrt(jnp.mean(t * t, axis=0, keepdims=True) + EPS) * gain
        sw = jnp.concatenate([y[16:32], y[0:16], y[48:64], y[32:48]], axis=0)
        return y * cos + sw * sin

    qg = qg_ref[...]
    kg = kg_ref[...]
    qa = proj(P1_QA, P1_KA)
    for hd in range(ATTN_HEADS):
        lo = hd * HEAD_DIM
        qta_ref[0, 0, lo:lo + HEAD_DIM, :] = (norm_rope(qa[lo:lo + HEAD_DIM], qg) * Q_SCALE).astype(bf16)
    ka = proj(P1_KA, P1_VA)
    kt = jnp.concatenate([norm_rope(ka[0:HEAD_DIM], kg), norm_rope(ka[HEAD_DIM:], kg)], axis=0)
    ka_ref[0] = kt.T.astype(bf16)
    vta_ref[0] = proj(P1_VA, P1_QB).astype(bf16)
    qtb_ref[0] = (proj(P1_QB, P1_KB) * Q_SCALE).astype(bf16)
    kbt = proj(P1_KB, P1_VB)
    for p in range(NA_HEADS // 2):
        kb_ref[0, p] = kbt[128 * p:128 * (p + 1)].T.astype(bf16)
    vtb_ref[0] = proj(P1_VB, P1_END).astype(bf16)


def _proj_call(x3, norm_g, w1t, qg, kg, cos_t, sin_t, tm):
    bx, lx, _ = x3.shape
    grid = (bx, lx // tm)
    out_shape = (
        jax.ShapeDtypeStruct((bx, lx // tm, ATTN_WIDTH, tm), bf16),
        jax.ShapeDtypeStruct((bx, lx, KV_WIDTH), bf16),
        jax.ShapeDtypeStruct((bx, KV_WIDTH, lx), bf16),
        jax.ShapeDtypeStruct((bx, NA_WIDTH, lx), bf16),
        jax.ShapeDtypeStruct((bx, NA_HEADS // 2, lx, 128), bf16),
        jax.ShapeDtypeStruct((bx, NA_WIDTH, lx), bf16),
    )
    return pl.pallas_call(
        _proj_kernel,
        grid=grid,
        in_specs=[
            pl.BlockSpec((1, tm, D_MODEL), lambda b, j: (b, j, 0)),
            pl.BlockSpec((1, D_MODEL), lambda b, j: (0, 0)),
            pl.BlockSpec((P1_END, D_MODEL), lambda b, j: (0, 0)),
            pl.BlockSpec((HEAD_DIM, 1), lambda b, j: (0, 0)),
            pl.BlockSpec((HEAD_DIM, 1), lambda b, j: (0, 0)),
            pl.BlockSpec((HEAD_DIM, tm), lambda b, j: (0, j)),
            pl.BlockSpec((HEAD_DIM, tm), lambda b, j: (0, j)),
        ],
        out_specs=(
            pl.BlockSpec((1, 1, ATTN_WIDTH, tm), lambda b, j: (b, j, 0, 0)),
            pl.BlockSpec((1, tm, KV_WIDTH), lambda b, j: (b, j, 0)),
            pl.BlockSpec((1, KV_WIDTH, tm), lambda b, j: (b, 0, j)),
            pl.BlockSpec((1, NA_WIDTH, tm), lambda b, j: (b, 0, j)),
            pl.BlockSpec((1, NA_HEADS // 2, tm, 128), lambda b, j: (b, 0, j, 0)),
            pl.BlockSpec((1, NA_WIDTH, tm), lambda b, j: (b, 0, j)),
        ),
        out_shape=out_shape,
        compiler_params=_params(("parallel", "parallel")),
        name="qkv_proj",
    )(x3, norm_g, w1t, qg, kg, cos_t, sin_t)


def _colmax8(x):
    return jnp.max(x.reshape(x.shape[0] // 8, 8, x.shape[1]), axis=0)


def _denominators_trusted(l_min, l_max):
    return jnp.logical_and(jnp.min(l_min) >= DENOM_FLOOR, jnp.max(l_max) <= DENOM_CEIL)


def _gqa_kernel(q_ref, k_ref, km_ref, vt_ref, vtm_ref, o_ref, s_ref, vx_ref, *, tq, n_qblk):
    g = pl.program_id(1)
    row = lax.broadcasted_iota(jnp.int32, (2 * HEAD_DIM, tq), 0)
    sel = (row >= HEAD_DIM) == (g == 1)
    meta_valid = row < N_META
    n_items = n_qblk * ATTN_GROUP
    n_chunks = SEQ // GQA_TK
    blocks = [(0, META_PAD)] + [(META_PAD + c * GQA_TK, GQA_TK) for c in range(n_chunks)]

    vx_ref[0:HEAD_DIM, 0:META_PAD] = vtm_ref[0]
    vx_ref[0:HEAD_DIM, META_PAD:] = vt_ref[0]
    vx_ref[HEAD_DIM:, :] = jnp.ones((GQA_ONES_ROWS, META_PAD + SEQ), bf16)

    def item_index(t):
        return t // ATTN_GROUP, pl.multiple_of((t % ATTN_GROUP) * HEAD_DIM, HEAD_DIM)

    def load_qext(t):
        qi, off = item_index(t)
        q = q_ref[0, qi, pl.ds(off, HEAD_DIM), :]
        return jnp.where(sel, jnp.concatenate([q, q], axis=0), jnp.zeros((), bf16))

    def score_block(qext, lo, n):
        if lo == 0:
            return jnp.where(meta_valid, _nn(km_ref[0], qext), NEG)
        return _nn(k_ref[0, lo - META_PAD:lo - META_PAD + n, :], qext)

    def store_item(t, acc):
        l = acc[HEAD_DIM:HEAD_DIM + 1]
        qi, off = item_index(t)
        o_ref[0, qi, pl.ds(off, HEAD_DIM), :] = (acc[0:HEAD_DIM] / l).astype(bf16)
        return l

    def fast_item(t, carry):
        l_min, l_max = carry
        qext = load_qext(t)
        acc, pending = None, None
        for lo, n in blocks + [(None, None)]:
            s = score_block(qext, lo, n) if lo is not None else None
            if pending is not None:
                s_prev, plo, pn = pending
                d = _nn(vx_ref[:, plo:plo + pn], jnp.exp2(s_prev).astype(bf16))
                acc = d if acc is None else acc + d
            pending = (s, lo, n)
        l = store_item(t, acc)
        return jnp.minimum(l_min, l), jnp.maximum(l_max, l)

    ones = jnp.full((1, tq), 1.0, f32)
    l_min, l_max = lax.fori_loop(0, n_items, fast_item, (ones, ones), unroll=4)

    def exact_item(t, carry):
        qext = load_qext(t)
        m8 = None
        for lo, n in blocks:
            s = score_block(qext, lo, n)
            s_ref[lo:lo + n, :] = s
            m8 = _colmax8(s) if m8 is None else jnp.maximum(m8, _colmax8(s))
        m = jnp.max(m8, axis=0, keepdims=True)
        acc = None
        for lo, n in blocks:
            d = _nn(vx_ref[:, lo:lo + n], jnp.exp2(s_ref[lo:lo + n, :] - m).astype(bf16))
            acc = d if acc is None else acc + d
        store_item(t, acc)
        return carry

    @pl.when(jnp.logical_not(_denominators_trusted(l_min, l_max)))
    def _():
        lax.fori_loop(0, n_items, exact_item, 0)


def _gqa_call(qt, k_real, k_meta, vt_real, vt_meta, *, meta_queries):
    b = k_real.shape[0]
    gw = ATTN_GROUP * HEAD_DIM
    if meta_queries:
        per_tile = TOKEN_TILE // META_PAD
        tq, n_qblk = META_PAD, 1
        q_map = lambda bi, g: (0, bi // per_tile, g, bi % per_tile)
    else:
        tq, n_qblk = TOKEN_TILE, SEQ // TOKEN_TILE
        q_map = lambda bi, g: (bi, 0, g, 0)
    n_keys = META_PAD + SEQ
    return pl.pallas_call(
        functools.partial(_gqa_kernel, tq=tq, n_qblk=n_qblk),
        grid=(b, ATTN_KV_HEADS),
        in_specs=[
            pl.BlockSpec((1, n_qblk, gw, tq), q_map),
            pl.BlockSpec((1, SEQ, KV_WIDTH), lambda bi, g: (bi, 0, 0)),
            pl.BlockSpec((1, META_PAD, KV_WIDTH), lambda bi, g: (0, bi, 0)),
            pl.BlockSpec((1, HEAD_DIM, SEQ), lambda bi, g: (bi, g, 0)),
            pl.BlockSpec((1, HEAD_DIM, META_PAD), lambda bi, g: (0, g, bi)),
        ],
        out_specs=pl.BlockSpec((1, n_qblk, gw, tq), q_map),
        out_shape=jax.ShapeDtypeStruct(qt.shape, bf16),
        scratch_shapes=[
            pltpu.VMEM((n_keys, tq), f32),
            pltpu.VMEM((HEAD_DIM + GQA_ONES_ROWS, n_keys), bf16),
        ],
        compiler_params=_params(("parallel", "parallel")),
        name="gqa_meta" if meta_queries else "gqa_attn",
    )(qt, k_real, k_meta, vt_real, vt_meta)


def _na_block_geometry(blk):
    r0 = blk * NA_QROWS
    kb = min(max(r0 - NA_KH // 2, 0), GRID_ROWS - NA_BAND_ROWS)
    typ = 0 if blk == 0 else (2 if blk == NA_NBLK - 1 else 1)
    return r0, kb, typ


def _na_row_window(r):
    rs = min(max(r - NA_KH // 2, 0), GRID_ROWS - NA_KH)
    return rs, rs + NA_KH


def _na_build_tables(rpb_ref, h, t_ref, tab_ref):
    kc = lax.broadcasted_iota(jnp.int32, (GRID_W, 128), 0)
    lane = lax.broadcasted_iota(jnp.int32, (GRID_W, 128), 1)
    c = lane % GRID_W
    rel = kc - c + (NA_KW - 1)
    cs = jnp.clip(c - NA_KW // 2, 0, GRID_W - NA_KW)
    cvalid = (kc >= cs) & (kc < cs + NA_KW)
    neg_tile = jnp.full((GRID_W, 128), NEG, f32)

    def build_a(a, carry):
        acc = neg_tile
        for r in range(2 * NA_KW - 1):
            acc = jnp.where(rel == r, rpb_ref[h, a, r] * LOG2E, acc)
        t_ref[a] = jnp.where(cvalid, acc, NEG)
        return carry

    lax.fori_loop(0, 2 * NA_KH - 1, build_a, 0)
    for typ, blk in enumerate((0, 1, NA_NBLK - 1)):
        r0, kb, _ = _na_block_geometry(blk)
        for i in range(NA_BAND_ROWS):
            kr = kb + i
            for jj in range(NA_QROWS // 2):
                halves = []
                for r in (r0 + 2 * jj, r0 + 2 * jj + 1):
                    lo, hi = _na_row_window(r)
                    halves.append(t_ref[kr - r + NA_KH - 1] if lo <= kr < hi else neg_tile)
                tab_ref[typ, i * GRID_W:(i + 1) * GRID_W, jj * 128:(jj + 1) * 128] = jnp.where(
                    lane < GRID_W, halves[0], halves[1])


def _na_kernel(rpb_ref, q_ref, k_ref, km_ref, vt_ref, vtm_ref, o_ref,
               t_ref, tab_ref, s_ref, vx_ref):
    @pl.when(pl.program_id(1) == 0)
    def _():
        _na_build_tables(rpb_ref, pl.program_id(0), t_ref, tab_ref)

    par = pl.program_id(0) % 2
    row = lax.broadcasted_iota(jnp.int32, (2 * HEAD_DIM, NA_QBLK), 0)
    sel = (row >= HEAD_DIM) == (par == 1)
    meta_valid = row < N_META

    vx_ref[0:HEAD_DIM, 0:META_PAD] = vtm_ref[0]
    vx_ref[0:HEAD_DIM, META_PAD:] = vt_ref[0]
    vx_ref[HEAD_DIM:, :] = jnp.ones((GQA_ONES_ROWS, META_PAD + SEQ), bf16)

    def band_start(blk):
        kb = jnp.clip(blk * NA_QROWS - NA_KH // 2, 0, GRID_ROWS - NA_BAND_ROWS)
        return pl.multiple_of(kb * GRID_W, 2 * GRID_W)

    def load_qext(blk):
        q = q_ref[0, :, pl.ds(pl.multiple_of(blk * NA_QBLK, NA_QBLK), NA_QBLK)]
        return jnp.where(sel, jnp.concatenate([q, q], axis=0), jnp.zeros((), bf16))

    def block_scores(blk, qext):
        typ = jnp.where(blk == 0, 0, jnp.where(blk == NA_NBLK - 1, 2, 1))
        sm = jnp.where(meta_valid, _nn(km_ref[0, 0], qext), NEG)
        s = _nn(k_ref[0, 0, pl.ds(band_start(blk), NA_BAND), :], qext) + tab_ref[typ]
        return sm, s

    def block_output(blk, pm, p):
        ks = pl.multiple_of(META_PAD + band_start(blk), META_PAD)
        acc = _nn(vx_ref[:, 0:META_PAD], pm) + _nn(vx_ref[:, pl.ds(ks, NA_BAND)], p)
        l = acc[HEAD_DIM:HEAD_DIM + 1]
        o_ref[0, :, pl.ds(pl.multiple_of(blk * NA_QBLK, NA_QBLK), NA_QBLK)] = (
            acc[0:HEAD_DIM] / l).astype(bf16)
        return l

    def fast_stage(u, carry):
        l_min, l_max = carry
        pending = None
        for j in range(NA_UNROLL + 1):
            if j < NA_UNROLL:
                blk = u * NA_UNROLL + j
                nxt = (blk,) + block_scores(blk, load_qext(blk))
            if pending is not None:
                pblk, sm, s = pending
                l = block_output(pblk, jnp.exp2(sm).astype(bf16), jnp.exp2(s).astype(bf16))
                l_min, l_max = jnp.minimum(l_min, l), jnp.maximum(l_max, l)
            pending = nxt
        return l_min, l_max

    ones = jnp.full((1, NA_QBLK), 1.0, f32)
    l_min, l_max = lax.fori_loop(0, NA_NBLK // NA_UNROLL, fast_stage, (ones, ones))

    def exact_block(blk, carry):
        sm, s = block_scores(blk, load_qext(blk))
        s_ref[0:META_PAD, :] = sm
        s_ref[META_PAD:, :] = s
        m = jnp.max(jnp.maximum(_colmax8(sm), _colmax8(s)), axis=0, keepdims=True)
        block_output(blk, jnp.exp2(s_ref[0:META_PAD, :] - m).astype(bf16),
                     jnp.exp2(s_ref[META_PAD:, :] - m).astype(bf16))
        return carry

    @pl.when(jnp.logical_not(_denominators_trusted(l_min, l_max)))
    def _():
        lax.fori_loop(0, NA_NBLK, exact_block, 0)


def _na_call(qt, k_real, k_meta, vt_real, vt_meta, rpb):
    b = qt.shape[0]
    return pl.pallas_call(
        _na_kernel,
        grid=(NA_HEADS, b),
        in_specs=[
            pl.BlockSpec(memory_space=pltpu.SMEM),
            pl.BlockSpec((1, HEAD_DIM, SEQ), lambda h, bi: (bi, h, 0)),
            pl.BlockSpec((1, 1, SEQ, 128), lambda h, bi: (bi, h // 2, 0, 0)),
            pl.BlockSpec((1, 1, META_PAD, 128), lambda h, bi: (0, h // 2, bi, 0)),
            pl.BlockSpec((1, HEAD_DIM, SEQ), lambda h, bi: (bi, h, 0)),
            pl.BlockSpec((1, HEAD_DIM, META_PAD), lambda h, bi: (0, h, bi)),
        ],
        out_specs=pl.BlockSpec((1, HEAD_DIM, SEQ), lambda h, bi: (bi, h, 0)),
        out_shape=jax.ShapeDtypeStruct(qt.shape, bf16),
        scratch_shapes=[
            pltpu.VMEM((2 * NA_KH - 1, GRID_W, 128), f32),
            pltpu.VMEM((3, NA_BAND, NA_QBLK), f32),
            pltpu.VMEM((META_PAD + NA_BAND, NA_QBLK), f32),
            pltpu.VMEM((HEAD_DIM + GQA_ONES_ROWS, META_PAD + SEQ), bf16),
        ],
        compiler_params=_params(("arbitrary", "arbitrary")),
        name="na_attn",
    )(rpb, qt, k_real, k_meta, vt_real, vt_meta)


def _na_meta_kernel(q_ref, km_ref, vtm_ref, o_ref):
    row = lax.broadcasted_iota(jnp.int32, (2 * HEAD_DIM, META_PAD), 0)
    meta_valid = row < N_META
    for h in range(NA_HEADS):
        lo = h * HEAD_DIM
        q = q_ref[0, lo:lo + HEAD_DIM, :]
        sel = (row >= HEAD_DIM) == (h % 2 == 1)
        qext = jnp.where(sel, jnp.concatenate([q, q], axis=0), jnp.zeros((), bf16))
        s = jnp.where(meta_valid, _nn(km_ref[0, h // 2], qext), NEG)
        m = jnp.max(s, axis=0, keepdims=True)
        p = jnp.exp2(s - m)
        l = jnp.sum(p, axis=0, keepdims=True)
        acc = _nn(vtm_ref[0, lo:lo + HEAD_DIM, :], p.astype(bf16))
        o_ref[0, lo:lo + HEAD_DIM, :] = (acc / l).astype(bf16)


def _na_meta_call(qt_meta, k_meta, vt_meta, b):
    return pl.pallas_call(
        _na_meta_kernel,
        grid=(b,),
        in_specs=[
            pl.BlockSpec((1, NA_WIDTH, META_PAD), lambda bi: (0, 0, bi)),
            pl.BlockSpec((1, NA_HEADS // 2, META_PAD, 128), lambda bi: (0, 0, bi, 0)),
            pl.BlockSpec((1, NA_WIDTH, META_PAD), lambda bi: (0, 0, bi)),
        ],
        out_specs=pl.BlockSpec((1, NA_WIDTH, META_PAD), lambda bi: (0, 0, bi)),
        out_shape=jax.ShapeDtypeStruct(qt_meta.shape, bf16),
        compiler_params=_params(("parallel",)),
        name="na_meta",
    )(qt_meta, k_meta, vt_meta)


def _out_kernel(x_ref, g_ref, ata_ref, atb_ref, w2_ref, woa_ref, wob_ref, wout_ref, fg_ref,
                o_ref, *, final):
    x = x_ref[0]
    h = _rms_rows(x, g_ref[...]).astype(bf16)

    def gated(at_ref, lo, hi):
        z = _nt(w2_ref[lo:hi, :], h)
        return (at_ref[...].reshape(z.shape).astype(f32) * (z * _sigmoid(z))).astype(bf16)

    ya = _nn(woa_ref[...], gated(ata_ref, P2_ZA, P2_ZB))
    yb = _nn(wob_ref[...], gated(atb_ref, P2_ZB, P2_GA))
    ga = _nt(w2_ref[P2_GA:P2_GB, :], h)
    gb = _nt(w2_ref[P2_GB:P2_END, :], h)
    mix = (_sigmoid(ga) * ya + _sigmoid(gb) * yb).astype(bf16)
    out = x + _tn(mix, wout_ref[...])
    if final:
        out = _rms_rows(out, fg_ref[...])
    o_ref[0] = out


def _out_call(x3, norm_g, at_a, at_b, w2t, woat, wobt, wout, final_g, tm, final):
    bx, lx, _ = x3.shape
    const = lambda b, j: (0, 0)
    return pl.pallas_call(
        functools.partial(_out_kernel, final=final),
        grid=(bx, lx // tm),
        in_specs=[
            pl.BlockSpec((1, tm, D_MODEL), lambda b, j: (b, j, 0)),
            pl.BlockSpec((1, D_MODEL), const),
            pl.BlockSpec((1, 1, ATTN_WIDTH, tm), lambda b, j: (b, j, 0, 0)),
            pl.BlockSpec((1, NA_WIDTH, tm), lambda b, j: (b, 0, j)),
            pl.BlockSpec((P2_END, D_MODEL), const),
            pl.BlockSpec((D_MODEL, ATTN_WIDTH), const),
            pl.BlockSpec((D_MODEL, NA_WIDTH), const),
            pl.BlockSpec((D_MODEL, D_MODEL), const),
            pl.BlockSpec((1, D_MODEL), const),
        ],
        out_specs=pl.BlockSpec((1, tm, D_MODEL), lambda b, j: (b, j, 0)),
        out_shape=jax.ShapeDtypeStruct(x3.shape, f32),
        compiler_params=_params(("parallel", "parallel")),
        name="gate_out_proj",
    )(x3, norm_g, at_a, at_b, w2t, woat, wobt, wout, final_g)


def _rope_tables():
    t = np.arange(SEQ)
    axis_dim = HEAD_DIM // 2
    inv = jnp.asarray(ROPE_THETA, f32) ** (-jnp.arange(0, axis_dim, 2, dtype=f32) / axis_dim)
    ang_r = jnp.asarray(t // GRID_W, f32)[None, :] * inv[:, None]
    ang_c = jnp.asarray(t % GRID_W, f32)[None, :] * inv[:, None]
    ang = jnp.concatenate([ang_r, ang_r, ang_c, ang_c], axis=0)
    sign = np.repeat(np.array([-1.0, 1.0, -1.0, 1.0], np.float32), axis_dim // 2)[:, None]
    return jnp.cos(ang), jnp.sin(ang) * sign


def kernel(x, meta_tokens, norm_g, w_in, q_norm_g, k_norm_g, na_rpb, w_o_attn, w_o_na, w_out,
           final_norm_g):
    b = x.shape[0]
    depth = w_in.shape[0]
    cos_r, sin_r = _rope_tables()
    cos_m = jnp.ones((HEAD_DIM, b * META_PAD), f32)
    sin_m = jnp.zeros((HEAD_DIM, b * META_PAD), f32)

    meta_pad = jnp.concatenate(
        [meta_tokens.astype(f32), jnp.zeros((META_PAD - N_META, D_MODEL), f32)], axis=0)
    xm = jnp.tile(meta_pad, (b, 1))[None]
    xr = x
    fg = final_norm_g.reshape(1, D_MODEL)

    for i in range(depth):
        wi = w_in[i]
        qa, ka, va, za, qb, kb, vb, zb, ga, gb = (
            wi[:, 0:512], wi[:, 512:640], wi[:, 640:768], wi[:, 768:1280], wi[:, 1280:1792],
            wi[:, 1792:2304], wi[:, 2304:2816], wi[:, 2816:3328], wi[:, 3328:4352], wi[:, 4352:5376])
        w1t = jnp.concatenate([qa, ka, va, qb, kb, vb], axis=1).T.astype(bf16)
        w2t = jnp.concatenate([za, zb, ga, gb], axis=1).T.astype(bf16)
        woat = w_o_attn[i].T.astype(bf16)
        wobt = w_o_na[i].T.astype(bf16)
        wout = w_out[i].astype(bf16)
        ng = norm_g[i].reshape(1, D_MODEL)
        qg = q_norm_g[i].reshape(HEAD_DIM, 1)
        kg = k_norm_g[i].reshape(HEAD_DIM, 1)
        last = i == depth - 1

        qta, k_a, vta, qtb, k_b, vtb = _proj_call(xr, ng, w1t, qg, kg, cos_r, sin_r, TOKEN_TILE)
        qta_m, k_a_m, vta_m, qtb_m, k_b_m, vtb_m = _proj_call(
            xm, ng, w1t, qg, kg, cos_m, sin_m, TOKEN_TILE)

        at_a = _gqa_call(qta, k_a, k_a_m, vta, vta_m, meta_queries=False)
        at_b = _na_call(qtb, k_b, k_b_m, vtb, vtb_m, na_rpb[i])
        xr_new = _out_call(xr, ng, at_a, at_b, w2t, woat, wobt, wout, fg, TOKEN_TILE, last)
        if not last:
            at_a_m = _gqa_call(qta_m, k_a, k_a_m, vta, vta_m, meta_queries=True)
            at_b_m = _na_meta_call(qtb_m, k_b_m, vtb_m, b)
            xm = _out_call(xm, ng, at_a_m, at_b_m, w2t, woat, wobt, wout, fg, TOKEN_TILE, False)
        xr = xr_new
    return xr
```

```python
import functools

import numpy as np
import jax
import jax.numpy as jnp
from jax import lax
from jax.experimental import pallas as pl
from jax.experimental.pallas import tpu as pltpu

D_MODEL = 1024
SEQ = 4096
GRID_W = 64
GRID_ROWS = SEQ // GRID_W
N_META = 16
META_PAD = 128
HEAD_DIM = 64
ATTN_HEADS = 8
ATTN_KV_HEADS = 2
ATTN_GROUP = ATTN_HEADS // ATTN_KV_HEADS
NA_HEADS = 8
NA_KH = 8
NA_KW = 16
ROPE_THETA = 10000.0
EPS = 1e-6
SCALE = HEAD_DIM ** -0.5
LOG2E = 1.4426950408889634
Q_SCALE = SCALE * LOG2E
NEG = -1e30

ATTN_WIDTH = ATTN_HEADS * HEAD_DIM
KV_WIDTH = ATTN_KV_HEADS * HEAD_DIM
NA_WIDTH = NA_HEADS * HEAD_DIM

P1_QA, P1_KA, P1_VA, P1_QB, P1_KB, P1_VB, P1_END = 0, 512, 640, 768, 1280, 1792, 2304
P2_ZA, P2_ZB, P2_GA, P2_GB, P2_END = 0, 512, 1024, 2048, 3072

NA_QROWS = 4
NA_QBLK = NA_QROWS * GRID_W
NA_BAND_ROWS = 12
NA_BAND = NA_BAND_ROWS * GRID_W
NA_NBLK = GRID_ROWS // NA_QROWS
NA_UNROLL = 16

TOKEN_TILE = 512
GQA_TK = 512
GQA_UNROLL = 4
GQA_ONES_ROWS = 16
DENOM_FLOOR = 2.0 ** -64
DENOM_CEIL = 2.0 ** 100
VMEM_LIMIT = 56 * 1024 * 1024

bf16 = jnp.bfloat16
f32 = jnp.float32


def _nn(a, b):
    return lax.dot_general(a, b, (((1,), (0,)), ((), ())), preferred_element_type=f32)


def _nt(a, b):
    return lax.dot_general(a, b, (((1,), (1,)), ((), ())), preferred_element_type=f32)


def _tn(a, b):
    return lax.dot_general(a, b, (((0,), (0,)), ((), ())), preferred_element_type=f32)


def _rms_rows(x, g):
    return x * lax.rsqrt(jnp.mean(x * x, axis=-1, keepdims=True) + EPS) * g


def _sigmoid(x):
    return 1.0 / (1.0 + jnp.exp(-x))


def _params(sem, flags=None):
    return pltpu.CompilerParams(dimension_semantics=sem, vmem_limit_bytes=VMEM_LIMIT, flags=flags)


def _proj_kernel(x_ref, g_ref, w_ref, qg_ref, kg_ref, cos_ref, sin_ref,
                 qta_ref, ka_ref, vta_ref, qtb_ref, kb_ref, vtb_ref):
    h = _rms_rows(x_ref[0], g_ref[...]).astype(bf16)
    cos = cos_ref[...]
    sin = sin_ref[...]

    def norm_rope(t, gain):
        y = t * lax.rsqrt(jnp.mean(t * t, axis=0, keepdims=True) + EPS) * gain
        sw = jnp.concatenate([y[16:32], y[0:16], y[48:64], y[32:48]], axis=0)
        return y * cos + sw * sin

    qg = qg_ref[...]
    kg = kg_ref[...]
    def proj(lo, hi):
        return _nt(w_ref[lo:hi, :], h)

    pa = proj(P1_QA, P1_QB)
    qa, ka, va = pa[P1_QA:P1_KA], pa[P1_KA:P1_VA], pa[P1_VA:P1_QB]
    qb = proj(P1_QB, P1_KB)
    kbt = proj(P1_KB, P1_VB)
    vb = proj(P1_VB, P1_END)
    for hd in range(ATTN_HEADS):
        lo = hd * HEAD_DIM
        qta_ref[0, 0, lo:lo + HEAD_DIM, :] = (norm_rope(qa[lo:lo + HEAD_DIM], qg) * Q_SCALE).astype(bf16)
    kt = jnp.concatenate([norm_rope(ka[0:HEAD_DIM], kg), norm_rope(ka[HEAD_DIM:], kg)], axis=0)
    ka_ref[0] = kt.T.astype(bf16)
    vta_ref[0] = va.astype(bf16)
    qtb_ref[0] = (qb * Q_SCALE).astype(bf16)
    for p in range(NA_HEADS // 2):
        kb_ref[0, p] = kbt[128 * p:128 * (p + 1)].T.astype(bf16)
    vtb_ref[0] = vb.astype(bf16)


def _proj_call(x3, norm_g, w1t, qg, kg, cos_t, sin_t, tm):
    bx, lx, _ = x3.shape
    grid = (bx, lx // tm)
    out_shape = (
        jax.ShapeDtypeStruct((bx, lx // tm, ATTN_WIDTH, tm), bf16),
        jax.ShapeDtypeStruct((bx, lx, KV_WIDTH), bf16),
        jax.ShapeDtypeStruct((bx, KV_WIDTH, lx), bf16),
        jax.ShapeDtypeStruct((bx, NA_WIDTH, lx), bf16),
        jax.ShapeDtypeStruct((bx, NA_HEADS // 2, lx, 128), bf16),
        jax.ShapeDtypeStruct((bx, NA_WIDTH, lx), bf16),
    )
    return pl.pallas_call(
        _proj_kernel,
        grid=grid,
        in_specs=[
            pl.BlockSpec((1, tm, D_MODEL), lambda b, j: (b, j, 0)),
            pl.BlockSpec((1, D_MODEL), lambda b, j: (0, 0)),
            pl.BlockSpec((P1_END, D_MODEL), lambda b, j: (0, 0)),
            pl.BlockSpec((HEAD_DIM, 1), lambda b, j: (0, 0)),
            pl.BlockSpec((HEAD_DIM, 1), lambda b, j: (0, 0)),
            pl.BlockSpec((HEAD_DIM, tm), lambda b, j: (0, j)),
            pl.BlockSpec((HEAD_DIM, tm), lambda b, j: (0, j)),
        ],
        out_specs=(
            pl.BlockSpec((1, 1, ATTN_WIDTH, tm), lambda b, j: (b, j, 0, 0)),
            pl.BlockSpec((1, tm, KV_WIDTH), lambda b, j: (b, j, 0)),
            pl.BlockSpec((1, KV_WIDTH, tm), lambda b, j: (b, 0, j)),
            pl.BlockSpec((1, NA_WIDTH, tm), lambda b, j: (b, 0, j)),
            pl.BlockSpec((1, NA_HEADS // 2, tm, 128), lambda b, j: (b, 0, j, 0)),
            pl.BlockSpec((1, NA_WIDTH, tm), lambda b, j: (b, 0, j)),
        ),
        out_shape=out_shape,
        compiler_params=_params(("parallel", "parallel")),
        name="qkv_proj",
    )(x3, norm_g, w1t, qg, kg, cos_t, sin_t)


def _colmax8(x):
    return jnp.max(x.reshape(x.shape[0] // 8, 8, x.shape[1]), axis=0)


def _denominators_trusted(l_min, l_max):
    return jnp.logical_and(jnp.min(l_min) >= DENOM_FLOOR, jnp.max(l_max) <= DENOM_CEIL)


def _gqa_kernel(q_ref, k_ref, km_ref, vt_ref, vtm_ref, o_ref, s_ref, vx_ref, *, tq, n_qblk, heads):
    g = pl.program_id(1)
    row = lax.broadcasted_iota(jnp.int32, (2 * HEAD_DIM, tq), 0)
    sel = (row >= HEAD_DIM) == (g == 1)
    meta_valid = row < N_META
    n_items = n_qblk * heads
    n_chunks = SEQ // GQA_TK
    blocks = [(0, META_PAD)] + [(META_PAD + c * GQA_TK, GQA_TK) for c in range(n_chunks)]

    vx_ref[0:HEAD_DIM, 0:META_PAD] = vtm_ref[0]
    vx_ref[0:HEAD_DIM, META_PAD:] = vt_ref[0]
    vx_ref[HEAD_DIM:, :] = jnp.ones((GQA_ONES_ROWS, META_PAD + SEQ), bf16)

    def item_index(t):
        return t // heads, pl.multiple_of((t % heads) * HEAD_DIM, HEAD_DIM)

    def load_qext(t):
        qi, off = item_index(t)
        q = q_ref[0, qi, pl.ds(off, HEAD_DIM), :]
        return jnp.where(sel, jnp.concatenate([q, q], axis=0), jnp.zeros((), bf16))

    def score_block(qext, lo, n):
        if lo == 0:
            return jnp.where(meta_valid, _nn(km_ref[0], qext), NEG)
        return _nn(k_ref[0, lo - META_PAD:lo - META_PAD + n, :], qext)

    def store_item(t, acc):
        l = acc[HEAD_DIM:HEAD_DIM + 1]
        qi, off = item_index(t)
        o_ref[0, qi, pl.ds(off, HEAD_DIM), :] = (acc[0:HEAD_DIM] / l).astype(bf16)
        return l

    def fast_item(t, carry):
        l_min, l_max = carry
        qext = load_qext(t)
        acc, pending = None, None
        for lo, n in blocks + [(None, None)]:
            s = score_block(qext, lo, n) if lo is not None else None
            if pending is not None:
                s_prev, plo, pn = pending
                d = _nn(vx_ref[:, plo:plo + pn], jnp.exp2(s_prev).astype(bf16))
                acc = d if acc is None else acc + d
            pending = (s, lo, n)
        l = store_item(t, acc)
        return jnp.minimum(l_min, l), jnp.maximum(l_max, l)

    ones = jnp.full((1, tq), 1.0, f32)
    l_min, l_max = lax.fori_loop(0, n_items, fast_item, (ones, ones), unroll=min(GQA_UNROLL, n_items))

    def exact_item(t, carry):
        qext = load_qext(t)
        m8 = None
        for lo, n in blocks:
            s = score_block(qext, lo, n)
            s_ref[lo:lo + n, :] = s
            m8 = _colmax8(s) if m8 is None else jnp.maximum(m8, _colmax8(s))
        m = jnp.max(m8, axis=0, keepdims=True)
        acc = None
        for lo, n in blocks:
            d = _nn(vx_ref[:, lo:lo + n], jnp.exp2(s_ref[lo:lo + n, :] - m).astype(bf16))
            acc = d if acc is None else acc + d
        store_item(t, acc)
        return carry

    @pl.when(jnp.logical_not(_denominators_trusted(l_min, l_max)))
    def _():
        lax.fori_loop(0, n_items, exact_item, 0)


def _pack_meta_queries(qt_meta, b):
    q = qt_meta.reshape(-1, ATTN_WIDTH, TOKEN_TILE // META_PAD, META_PAD)
    q = q.transpose(0, 2, 1, 3).reshape(b, ATTN_KV_HEADS, ATTN_GROUP, HEAD_DIM, META_PAD)
    q = q[..., :N_META].transpose(0, 1, 3, 2, 4).reshape(b, KV_WIDTH, ATTN_GROUP * N_META)
    return jnp.pad(q, ((0, 0), (0, 0), (0, META_PAD - ATTN_GROUP * N_META)))[:, None]


def _unpack_meta_outputs(o_packed, like):
    b = o_packed.shape[0]
    o = o_packed[:, 0, :, :ATTN_GROUP * N_META].reshape(b, ATTN_KV_HEADS, HEAD_DIM, ATTN_GROUP, N_META)
    o = o.transpose(0, 1, 3, 2, 4).reshape(b, ATTN_WIDTH, N_META)
    o = jnp.pad(o, ((0, 0), (0, 0), (0, META_PAD - N_META)))
    per_tile = TOKEN_TILE // META_PAD
    o = o.reshape(b // per_tile, per_tile, ATTN_WIDTH, META_PAD).transpose(0, 2, 1, 3)
    return o.reshape(like.shape)


def _gqa_call(qt, k_real, k_meta, vt_real, vt_meta, *, meta_queries):
    b = k_real.shape[0]
    if meta_queries:
        tq, n_qblk, heads = META_PAD, 1, 1
    else:
        tq, n_qblk, heads = TOKEN_TILE, SEQ // TOKEN_TILE, ATTN_GROUP
    gw = heads * HEAD_DIM
    q_map = lambda bi, g: (bi, 0, g, 0)
    n_keys = META_PAD + SEQ
    return pl.pallas_call(
        functools.partial(_gqa_kernel, tq=tq, n_qblk=n_qblk, heads=heads),
        grid=(b, ATTN_KV_HEADS),
        in_specs=[
            pl.BlockSpec((1, n_qblk, gw, tq), q_map),
            pl.BlockSpec((1, SEQ, KV_WIDTH), lambda bi, g: (bi, 0, 0)),
            pl.BlockSpec((1, META_PAD, KV_WIDTH), lambda bi, g: (0, bi, 0)),
            pl.BlockSpec((1, HEAD_DIM, SEQ), lambda bi, g: (bi, g, 0)),
            pl.BlockSpec((1, HEAD_DIM, META_PAD), lambda bi, g: (0, g, bi)),
        ],
        out_specs=pl.BlockSpec((1, n_qblk, gw, tq), q_map),
        out_shape=jax.ShapeDtypeStruct(qt.shape, bf16),
        scratch_shapes=[
            pltpu.VMEM((n_keys, tq), f32),
            pltpu.VMEM((HEAD_DIM + GQA_ONES_ROWS, n_keys), bf16),
        ],
        compiler_params=_params(("parallel", "parallel")),
        name="gqa_meta" if meta_queries else "gqa_attn",
    )(qt, k_real, k_meta, vt_real, vt_meta)


def _na_block_geometry(blk):
    r0 = blk * NA_QROWS
    kb = min(max(r0 - NA_KH // 2, 0), GRID_ROWS - NA_BAND_ROWS)
    typ = 0 if blk == 0 else (2 if blk == NA_NBLK - 1 else 1)
    return r0, kb, typ


def _na_row_window(r):
    rs = min(max(r - NA_KH // 2, 0), GRID_ROWS - NA_KH)
    return rs, rs + NA_KH


def _na_build_tables(rpb_ref, h, t_ref, tab_ref):
    kc = lax.broadcasted_iota(jnp.int32, (GRID_W, 128), 0)
    lane = lax.broadcasted_iota(jnp.int32, (GRID_W, 128), 1)
    c = lane % GRID_W
    rel = kc - c + (NA_KW - 1)
    cs = jnp.clip(c - NA_KW // 2, 0, GRID_W - NA_KW)
    cvalid = (kc >= cs) & (kc < cs + NA_KW)
    neg_tile = jnp.full((GRID_W, 128), NEG, f32)

    def build_a(a, carry):
        acc = neg_tile
        for r in range(2 * NA_KW - 1):
            acc = jnp.where(rel == r, rpb_ref[h, a, r] * LOG2E, acc)
        t_ref[a] = jnp.where(cvalid, acc, NEG)
        return carry

    lax.fori_loop(0, 2 * NA_KH - 1, build_a, 0)
    for typ, blk in enumerate((0, 1, NA_NBLK - 1)):
        r0, kb, _ = _na_block_geometry(blk)
        for i in range(NA_BAND_ROWS):
            kr = kb + i
            for jj in range(NA_QROWS // 2):
                halves = []
                for r in (r0 + 2 * jj, r0 + 2 * jj + 1):
                    lo, hi = _na_row_window(r)
                    halves.append(t_ref[kr - r + NA_KH - 1] if lo <= kr < hi else neg_tile)
                tab_ref[typ, i * GRID_W:(i + 1) * GRID_W, jj * 128:(jj + 1) * 128] = jnp.where(
                    lane < GRID_W, halves[0], halves[1])


def _na_kernel(rpb_ref, q_ref, k_ref, km_ref, vt_ref, vtm_ref, o_ref,
               t_ref, tab_ref, s_ref, vx_ref):
    @pl.when(pl.program_id(1) == 0)
    def _():
        _na_build_tables(rpb_ref, pl.program_id(0), t_ref, tab_ref)

    par = pl.program_id(0) % 2
    row = lax.broadcasted_iota(jnp.int32, (2 * HEAD_DIM, NA_QBLK), 0)
    sel = (row >= HEAD_DIM) == (par == 1)
    meta_valid = row < N_META

    vx_ref[0:HEAD_DIM, 0:META_PAD] = vtm_ref[0]
    vx_ref[0:HEAD_DIM, META_PAD:] = vt_ref[0]
    vx_ref[HEAD_DIM:, :] = jnp.ones((GQA_ONES_ROWS, META_PAD + SEQ), bf16)

    def band_start(blk):
        kb = jnp.clip(blk * NA_QROWS - NA_KH // 2, 0, GRID_ROWS - NA_BAND_ROWS)
        return pl.multiple_of(kb * GRID_W, 2 * GRID_W)

    def load_qext(blk):
        q = q_ref[0, :, pl.ds(pl.multiple_of(blk * NA_QBLK, NA_QBLK), NA_QBLK)]
        return jnp.where(sel, jnp.concatenate([q, q], axis=0), jnp.zeros((), bf16))

    def block_scores(blk, qext):
        typ = jnp.where(blk == 0, 0, jnp.where(blk == NA_NBLK - 1, 2, 1))
        sm = jnp.where(meta_valid, _nn(km_ref[0, 0], qext), NEG)
        s = _nn(k_ref[0, 0, pl.ds(band_start(blk), NA_BAND), :], qext) + tab_ref[typ]
        return sm, s

    def block_output(blk, pm, p):
        ks = pl.multiple_of(META_PAD + band_start(blk), META_PAD)
        acc = _nn(vx_ref[:, 0:META_PAD], pm) + _nn(vx_ref[:, pl.ds(ks, NA_BAND)], p)
        l = acc[HEAD_DIM:HEAD_DIM + 1]
        o_ref[0, :, pl.ds(pl.multiple_of(blk * NA_QBLK, NA_QBLK), NA_QBLK)] = (
            acc[0:HEAD_DIM] / l).astype(bf16)
        return l

    def fast_stage(u, carry):
        l_min, l_max = carry
        pending = None
        for j in range(NA_UNROLL + 1):
            if j < NA_UNROLL:
                blk = u * NA_UNROLL + j
                nxt = (blk,) + block_scores(blk, load_qext(blk))
            if pending is not None:
                pblk, sm, s = pending
                l = block_output(pblk, jnp.exp2(sm).astype(bf16), jnp.exp2(s).astype(bf16))
                l_min, l_max = jnp.minimum(l_min, l), jnp.maximum(l_max, l)
            pending = nxt
        return l_min, l_max

    ones = jnp.full((1, NA_QBLK), 1.0, f32)
    l_min, l_max = lax.fori_loop(0, NA_NBLK // NA_UNROLL, fast_stage, (ones, ones))

    def exact_block(blk, carry):
        sm, s = block_scores(blk, load_qext(blk))
        s_ref[0:META_PAD, :] = sm
        s_ref[META_PAD:, :] = s
        m = jnp.max(jnp.maximum(_colmax8(sm), _colmax8(s)), axis=0, keepdims=True)
        block_output(blk, jnp.exp2(s_ref[0:META_PAD, :] - m).astype(bf16),
                     jnp.exp2(s_ref[META_PAD:, :] - m).astype(bf16))
        return carry

    @pl.when(jnp.logical_not(_denominators_trusted(l_min, l_max)))
    def _():
        lax.fori_loop(0, NA_NBLK, exact_block, 0)


def _na_call(qt, k_real, k_meta, vt_real, vt_meta, rpb):
    b = qt.shape[0]
    return pl.pallas_call(
        _na_kernel,
        grid=(NA_HEADS, b),
        in_specs=[
            pl.BlockSpec(memory_space=pltpu.SMEM),
            pl.BlockSpec((1, HEAD_DIM, SEQ), lambda h, bi: (bi, h, 0)),
            pl.BlockSpec((1, 1, SEQ, 128), lambda h, bi: (bi, h // 2, 0, 0)),
            pl.BlockSpec((1, 1, META_PAD, 128), lambda h, bi: (0, h // 2, bi, 0)),
            pl.BlockSpec((1, HEAD_DIM, SEQ), lambda h, bi: (bi, h, 0)),
            pl.BlockSpec((1, HEAD_DIM, META_PAD), lambda h, bi: (0, h, bi)),
        ],
        out_specs=pl.BlockSpec((1, HEAD_DIM, SEQ), lambda h, bi: (bi, h, 0)),
        out_shape=jax.ShapeDtypeStruct(qt.shape, bf16),
        scratch_shapes=[
            pltpu.VMEM((2 * NA_KH - 1, GRID_W, 128), f32),
            pltpu.VMEM((3, NA_BAND, NA_QBLK), f32),
            pltpu.VMEM((META_PAD + NA_BAND, NA_QBLK), f32),
            pltpu.VMEM((HEAD_DIM + GQA_ONES_ROWS, META_PAD + SEQ), bf16),
        ],
        compiler_params=_params(("arbitrary", "arbitrary")),
        name="na_attn",
    )(rpb, qt, k_real, k_meta, vt_real, vt_meta)


def _na_meta_kernel(q_ref, km_ref, vtm_ref, o_ref):
    row = lax.broadcasted_iota(jnp.int32, (2 * HEAD_DIM, META_PAD), 0)
    meta_valid = row < N_META
    for h in range(NA_HEADS):
        lo = h * HEAD_DIM
        q = q_ref[0, lo:lo + HEAD_DIM, :]
        sel = (row >= HEAD_DIM) == (h % 2 == 1)
        qext = jnp.where(sel, jnp.concatenate([q, q], axis=0), jnp.zeros((), bf16))
        s = jnp.where(meta_valid, _nn(km_ref[0, h // 2], qext), NEG)
        m = jnp.max(s, axis=0, keepdims=True)
        p = jnp.exp2(s - m)
        l = jnp.sum(p, axis=0, keepdims=True)
        acc = _nn(vtm_ref[0, lo:lo + HEAD_DIM, :], p.astype(bf16))
        o_ref[0, lo:lo + HEAD_DIM, :] = (acc / l).astype(bf16)


def _na_meta_call(qt_meta, k_meta, vt_meta, b):
    return pl.pallas_call(
        _na_meta_kernel,
        grid=(b,),
        in_specs=[
            pl.BlockSpec((1, NA_WIDTH, META_PAD), lambda bi: (0, 0, bi)),
            pl.BlockSpec((1, NA_HEADS // 2, META_PAD, 128), lambda bi: (0, 0, bi, 0)),
            pl.BlockSpec((1, NA_WIDTH, META_PAD), lambda bi: (0, 0, bi)),
        ],
        out_specs=pl.BlockSpec((1, NA_WIDTH, META_PAD), lambda bi: (0, 0, bi)),
        out_shape=jax.ShapeDtypeStruct(qt_meta.shape, bf16),
        compiler_params=_params(("parallel",)),
        name="na_meta",
    )(qt_meta, k_meta, vt_meta)


def _out_kernel(x_ref, g_ref, ata_ref, atb_ref, w2_ref, woa_ref, wob_ref, wout_ref, fg_ref,
                o_ref, *, final):
    x = x_ref[0]
    h = _rms_rows(x, g_ref[...]).astype(bf16)

    def gated(at_ref, z):
        return (at_ref[...].reshape(z.shape).astype(f32) * (z * _sigmoid(z))).astype(bf16)

    za = _nt(w2_ref[P2_ZA:P2_ZB, :], h)
    zb = _nt(w2_ref[P2_ZB:P2_GA, :], h)
    ga = _nt(w2_ref[P2_GA:P2_GB, :], h)
    gb = _nt(w2_ref[P2_GB:P2_END, :], h)
    ya = _nn(woa_ref[...], gated(ata_ref, za))
    yb = _nn(wob_ref[...], gated(atb_ref, zb))
    mix = (_sigmoid(ga) * ya + _sigmoid(gb) * yb).astype(bf16)
    out = x + _tn(mix, wout_ref[...])
    if final:
        out = _rms_rows(out, fg_ref[...])
    o_ref[0] = out


def _out_call(x3, norm_g, at_a, at_b, w2t, woat, wobt, wout, final_g, tm, final):
    bx, lx, _ = x3.shape
    const = lambda b, j: (0, 0)
    return pl.pallas_call(
        functools.partial(_out_kernel, final=final),
        grid=(bx, lx // tm),
        in_specs=[
            pl.BlockSpec((1, tm, D_MODEL), lambda b, j: (b, j, 0)),
            pl.BlockSpec((1, D_MODEL), const),
            pl.BlockSpec((1, 1, ATTN_WIDTH, tm), lambda b, j: (b, j, 0, 0)),
            pl.BlockSpec((1, NA_WIDTH, tm), lambda b, j: (b, 0, j)),
            pl.BlockSpec((P2_END, D_MODEL), const),
            pl.BlockSpec((D_MODEL, ATTN_WIDTH), const),
            pl.BlockSpec((D_MODEL, NA_WIDTH), const),
            pl.BlockSpec((D_MODEL, D_MODEL), const),
            pl.BlockSpec((1, D_MODEL), const),
        ],
        out_specs=pl.BlockSpec((1, tm, D_MODEL), lambda b, j: (b, j, 0)),
        out_shape=jax.ShapeDtypeStruct(x3.shape, f32),
        compiler_params=_params(("parallel", "parallel")),
        name="gate_out_proj",
    )(x3, norm_g, at_a, at_b, w2t, woat, wobt, wout, final_g)


def _rope_tables():
    t = np.arange(SEQ)
    axis_dim = HEAD_DIM // 2
    inv = jnp.asarray(ROPE_THETA, f32) ** (-jnp.arange(0, axis_dim, 2, dtype=f32) / axis_dim)
    ang_r = jnp.asarray(t // GRID_W, f32)[None, :] * inv[:, None]
    ang_c = jnp.asarray(t % GRID_W, f32)[None, :] * inv[:, None]
    ang = jnp.concatenate([ang_r, ang_r, ang_c, ang_c], axis=0)
    sign = np.repeat(np.array([-1.0, 1.0, -1.0, 1.0], np.float32), axis_dim // 2)[:, None]
    return jnp.cos(ang), jnp.sin(ang) * sign


def kernel(x, meta_tokens, norm_g, w_in, q_norm_g, k_norm_g, na_rpb, w_o_attn, w_o_na, w_out,
           final_norm_g):
    b = x.shape[0]
    depth = w_in.shape[0]
    cos_r, sin_r = _rope_tables()
    cos_m = jnp.ones((HEAD_DIM, b * META_PAD), f32)
    sin_m = jnp.zeros((HEAD_DIM, b * META_PAD), f32)

    meta_pad = jnp.concatenate(
        [meta_tokens.astype(f32), jnp.zeros((META_PAD - N_META, D_MODEL), f32)], axis=0)
    xm = jnp.tile(meta_pad, (b, 1))[None]
    xr = x
    fg = final_norm_g.reshape(1, D_MODEL)

    for i in range(depth):
        wi = w_in[i]
        qa, ka, va, za, qb, kb, vb, zb, ga, gb = (
            wi[:, 0:512], wi[:, 512:640], wi[:, 640:768], wi[:, 768:1280], wi[:, 1280:1792],
            wi[:, 1792:2304], wi[:, 2304:2816], wi[:, 2816:3328], wi[:, 3328:4352], wi[:, 4352:5376])
        w1t = jnp.concatenate([qa, ka, va, qb, kb, vb], axis=1).T.astype(bf16)
        w2t = jnp.concatenate([za, zb, ga, gb], axis=1).T.astype(bf16)
        woat = w_o_attn[i].T.astype(bf16)
        wobt = w_o_na[i].T.astype(bf16)
        wout = w_out[i].astype(bf16)
        ng = norm_g[i].reshape(1, D_MODEL)
        qg = q_norm_g[i].reshape(HEAD_DIM, 1)
        kg = k_norm_g[i].reshape(HEAD_DIM, 1)
        last = i == depth - 1

        qta, k_a, vta, qtb, k_b, vtb = _proj_call(xr, ng, w1t, qg, kg, cos_r, sin_r, TOKEN_TILE)
        qta_m, k_a_m, vta_m, qtb_m, k_b_m, vtb_m = _proj_call(
            xm, ng, w1t, qg, kg, cos_m, sin_m, TOKEN_TILE)

        at_a = _gqa_call(qta, k_a, k_a_m, vta, vta_m, meta_queries=False)
        at_b = _na_call(qtb, k_b, k_b_m, vtb, vtb_m, na_rpb[i])
        xr_new = _out_call(xr, ng, at_a, at_b, w2t, woat, wobt, wout, fg, TOKEN_TILE, last)
        if not last:
            at_a_m = _unpack_meta_outputs(
                _gqa_call(_pack_meta_queries(qta_m, b), k_a, k_a_m, vta, vta_m, meta_queries=True),
                qta_m)
            at_b_m = _na_meta_call(qtb_m, k_b_m, vtb_m, b)
            xm = _out_call(xm, ng, at_a_m, at_b_m, w2t, woat, wobt, wout, fg, TOKEN_TILE, False)
        xr = xr_new
    return xr
```

```python
import functools

import numpy as np
import jax
import jax.numpy as jnp
from jax import lax
from jax.experimental import pallas as pl
from jax.experimental.pallas import tpu as pltpu

D_MODEL = 1024
SEQ = 4096
GRID_W = 64
GRID_ROWS = SEQ // GRID_W
N_META = 16
META_PAD = 128
HEAD_DIM = 64
ATTN_HEADS = 8
ATTN_KV_HEADS = 2
ATTN_GROUP = ATTN_HEADS // ATTN_KV_HEADS
NA_HEADS = 8
NA_KH = 8
NA_KW = 16
ROPE_THETA = 10000.0
EPS = 1e-6
SCALE = HEAD_DIM ** -0.5
LOG2E = 1.4426950408889634
Q_SCALE = SCALE * LOG2E
NEG = -1e30

ATTN_WIDTH = ATTN_HEADS * HEAD_DIM
KV_WIDTH = ATTN_KV_HEADS * HEAD_DIM
NA_WIDTH = NA_HEADS * HEAD_DIM

P1_QA, P1_KA, P1_VA, P1_QB, P1_KB, P1_VB, P1_END = 0, 512, 640, 768, 1280, 1792, 2304
P2_ZA, P2_ZB, P2_GA, P2_GB, P2_END = 0, 512, 1024, 2048, 3072

NA_QROWS = 4
NA_QBLK = NA_QROWS * GRID_W
NA_BAND_ROWS = 12
NA_BAND = NA_BAND_ROWS * GRID_W
NA_NBLK = GRID_ROWS // NA_QROWS
NA_UNROLL = 16

TOKEN_TILE = 512
GQA_TK = 512
GQA_UNROLL = 4
GQA_ONES_ROWS = 16
DENOM_FLOOR = 2.0 ** -64
DENOM_CEIL = 2.0 ** 100
VMEM_LIMIT = 56 * 1024 * 1024

bf16 = jnp.bfloat16
f32 = jnp.float32


def _nn(a, b):
    return lax.dot_general(a, b, (((1,), (0,)), ((), ())), preferred_element_type=f32)


def _nt(a, b):
    return lax.dot_general(a, b, (((1,), (1,)), ((), ())), preferred_element_type=f32)


def _tn(a, b):
    return lax.dot_general(a, b, (((0,), (0,)), ((), ())), preferred_element_type=f32)


def _rms_rows(x, g):
    return x * lax.rsqrt(jnp.mean(x * x, axis=-1, keepdims=True) + EPS) * g


def _sigmoid(x):
    return 1.0 / (1.0 + jnp.exp(-x))


def _params(sem, flags=None):
    return pltpu.CompilerParams(dimension_semantics=sem, vmem_limit_bytes=VMEM_LIMIT, flags=flags)


def _proj_kernel(x_ref, g_ref, w_ref, qg_ref, kg_ref, cos_ref, sin_ref,
                 qta_ref, ka_ref, vta_ref, qtb_ref, kb_ref, vtb_ref):
    h = _rms_rows(x_ref[0], g_ref[...]).astype(bf16)
    cos = cos_ref[...]
    sin = sin_ref[...]

    def norm_rope(t, gain):
        y = t * lax.rsqrt(jnp.mean(t * t, axis=0, keepdims=True) + EPS) * gain
        sw = jnp.concatenate([y[16:32], y[0:16], y[48:64], y[32:48]], axis=0)
        return y * cos + sw * sin

    qg = qg_ref[...]
    kg = kg_ref[...]
    def proj(lo, hi):
        return _nt(w_ref[lo:hi, :], h)

    pa = proj(P1_QA, P1_QB)
    qa, ka, va = pa[P1_QA:P1_KA], pa[P1_KA:P1_VA], pa[P1_VA:P1_QB]
    qb = proj(P1_QB, P1_KB)
    kbt = proj(P1_KB, P1_VB)
    vb = proj(P1_VB, P1_END)
    for hd in range(ATTN_HEADS):
        lo = hd * HEAD_DIM
        qta_ref[0, 0, lo:lo + HEAD_DIM, :] = (norm_rope(qa[lo:lo + HEAD_DIM], qg) * Q_SCALE).astype(bf16)
    kt = jnp.concatenate([norm_rope(ka[0:HEAD_DIM], kg), norm_rope(ka[HEAD_DIM:], kg)], axis=0)
    ka_ref[0] = kt.T.astype(bf16)
    vta_ref[0] = va.astype(bf16)
    qtb_ref[0] = (qb * Q_SCALE).astype(bf16)
    for p in range(NA_HEADS // 2):
        kb_ref[0, p] = kbt[128 * p:128 * (p + 1)].T.astype(bf16)
    vtb_ref[0] = vb.astype(bf16)


def _proj_call(x3, norm_g, w1t, qg, kg, cos_t, sin_t, tm):
    bx, lx, _ = x3.shape
    grid = (bx, lx // tm)
    out_shape = (
        jax.ShapeDtypeStruct((bx, lx // tm, ATTN_WIDTH, tm), bf16),
        jax.ShapeDtypeStruct((bx, lx, KV_WIDTH), bf16),
        jax.ShapeDtypeStruct((bx, KV_WIDTH, lx), bf16),
        jax.ShapeDtypeStruct((bx, NA_WIDTH, lx), bf16),
        jax.ShapeDtypeStruct((bx, NA_HEADS // 2, lx, 128), bf16),
        jax.ShapeDtypeStruct((bx, NA_WIDTH, lx), bf16),
    )
    return pl.pallas_call(
        _proj_kernel,
        grid=grid,
        in_specs=[
            pl.BlockSpec((1, tm, D_MODEL), lambda b, j: (b, j, 0)),
            pl.BlockSpec((1, D_MODEL), lambda b, j: (0, 0)),
            pl.BlockSpec((P1_END, D_MODEL), lambda b, j: (0, 0)),
            pl.BlockSpec((HEAD_DIM, 1), lambda b, j: (0, 0)),
            pl.BlockSpec((HEAD_DIM, 1), lambda b, j: (0, 0)),
            pl.BlockSpec((HEAD_DIM, tm), lambda b, j: (0, j)),
            pl.BlockSpec((HEAD_DIM, tm), lambda b, j: (0, j)),
        ],
        out_specs=(
            pl.BlockSpec((1, 1, ATTN_WIDTH, tm), lambda b, j: (b, j, 0, 0)),
            pl.BlockSpec((1, tm, KV_WIDTH), lambda b, j: (b, j, 0)),
            pl.BlockSpec((1, KV_WIDTH, tm), lambda b, j: (b, 0, j)),
            pl.BlockSpec((1, NA_WIDTH, tm), lambda b, j: (b, 0, j)),
            pl.BlockSpec((1, NA_HEADS // 2, tm, 128), lambda b, j: (b, 0, j, 0)),
            pl.BlockSpec((1, NA_WIDTH, tm), lambda b, j: (b, 0, j)),
        ),
        out_shape=out_shape,
        compiler_params=_params(("parallel", "parallel")),
        name="qkv_proj",
    )(x3, norm_g, w1t, qg, kg, cos_t, sin_t)


def _colmax8(x):
    return jnp.max(x.reshape(x.shape[0] // 8, 8, x.shape[1]), axis=0)


def _denominators_trusted(l_min, l_max):
    return jnp.logical_and(jnp.min(l_min) >= DENOM_FLOOR, jnp.max(l_max) <= DENOM_CEIL)


def _gqa_kernel(q_ref, k_ref, km_ref, vt_ref, vtm_ref, o_ref, s_ref, vx_ref, *, tq, n_qblk, heads):
    g = pl.program_id(1)
    row = lax.broadcasted_iota(jnp.int32, (2 * HEAD_DIM, tq), 0)
    sel = (row >= HEAD_DIM) == (g == 1)
    n_items = n_qblk * heads
    n_chunks = SEQ // GQA_TK
    blocks = [(0, N_META)] + [(META_PAD + c * GQA_TK, GQA_TK) for c in range(n_chunks)]

    vx_ref[0:HEAD_DIM, 0:META_PAD] = vtm_ref[0]
    vx_ref[0:HEAD_DIM, META_PAD:] = vt_ref[0]
    vx_ref[HEAD_DIM:, :] = jnp.ones((GQA_ONES_ROWS, META_PAD + SEQ), bf16)

    def item_index(t):
        return t // heads, pl.multiple_of((t % heads) * HEAD_DIM, HEAD_DIM)

    def load_qext(t):
        qi, off = item_index(t)
        q = q_ref[0, qi, pl.ds(off, HEAD_DIM), :]
        return jnp.where(sel, jnp.concatenate([q, q], axis=0), jnp.zeros((), bf16))

    def score_block(qext, lo, n):
        if lo == 0:
            return _nn(km_ref[0, 0:n, :], qext)
        return _nn(k_ref[0, lo - META_PAD:lo - META_PAD + n, :], qext)

    def store_item(t, acc):
        l = acc[HEAD_DIM:HEAD_DIM + 1]
        qi, off = item_index(t)
        o_ref[0, qi, pl.ds(off, HEAD_DIM), :] = (acc[0:HEAD_DIM] / l).astype(bf16)
        return l

    def fast_item(t, carry):
        l_min, l_max = carry
        qext = load_qext(t)
        acc, pending = None, None
        for lo, n in blocks + [(None, None)]:
            s = score_block(qext, lo, n) if lo is not None else None
            if pending is not None:
                s_prev, plo, pn = pending
                d = _nn(vx_ref[:, plo:plo + pn], jnp.exp2(s_prev).astype(bf16))
                acc = d if acc is None else acc + d
            pending = (s, lo, n)
        l = store_item(t, acc)
        return jnp.minimum(l_min, l), jnp.maximum(l_max, l)

    ones = jnp.full((1, tq), 1.0, f32)
    l_min, l_max = lax.fori_loop(0, n_items, fast_item, (ones, ones), unroll=min(GQA_UNROLL, n_items))

    def exact_item(t, carry):
        qext = load_qext(t)
        m8 = None
        for lo, n in blocks:
            s = score_block(qext, lo, n)
            s_ref[lo:lo + n, :] = s
            m8 = _colmax8(s) if m8 is None else jnp.maximum(m8, _colmax8(s))
        m = jnp.max(m8, axis=0, keepdims=True)
        acc = None
        for lo, n in blocks:
            d = _nn(vx_ref[:, lo:lo + n], jnp.exp2(s_ref[lo:lo + n, :] - m).astype(bf16))
            acc = d if acc is None else acc + d
        store_item(t, acc)
        return carry

    @pl.when(jnp.logical_not(_denominators_trusted(l_min, l_max)))
    def _():
        lax.fori_loop(0, n_items, exact_item, 0)


def _pack_meta_queries(qt_meta, b):
    q = qt_meta.reshape(-1, ATTN_WIDTH, TOKEN_TILE // META_PAD, META_PAD)
    q = q.transpose(0, 2, 1, 3).reshape(b, ATTN_KV_HEADS, ATTN_GROUP, HEAD_DIM, META_PAD)
    q = q[..., :N_META].transpose(0, 1, 3, 2, 4).reshape(b, KV_WIDTH, ATTN_GROUP * N_META)
    return jnp.pad(q, ((0, 0), (0, 0), (0, META_PAD - ATTN_GROUP * N_META)))[:, None]


def _unpack_meta_outputs(o_packed, like):
    b = o_packed.shape[0]
    o = o_packed[:, 0, :, :ATTN_GROUP * N_META].reshape(b, ATTN_KV_HEADS, HEAD_DIM, ATTN_GROUP, N_META)
    o = o.transpose(0, 1, 3, 2, 4).reshape(b, ATTN_WIDTH, N_META)
    o = jnp.pad(o, ((0, 0), (0, 0), (0, META_PAD - N_META)))
    per_tile = TOKEN_TILE // META_PAD
    o = o.reshape(b // per_tile, per_tile, ATTN_WIDTH, META_PAD).transpose(0, 2, 1, 3)
    return o.reshape(like.shape)


def _gqa_call(qt, k_real, k_meta, vt_real, vt_meta, *, meta_queries):
    b = k_real.shape[0]
    if meta_queries:
        tq, n_qblk, heads = META_PAD, 1, 1
    else:
        tq, n_qblk, heads = TOKEN_TILE, SEQ // TOKEN_TILE, ATTN_GROUP
    gw = heads * HEAD_DIM
    q_map = lambda bi, g: (bi, 0, g, 0)
    n_keys = META_PAD + SEQ
    return pl.pallas_call(
        functools.partial(_gqa_kernel, tq=tq, n_qblk=n_qblk, heads=heads),
        grid=(b, ATTN_KV_HEADS),
        in_specs=[
            pl.BlockSpec((1, n_qblk, gw, tq), q_map),
            pl.BlockSpec((1, SEQ, KV_WIDTH), lambda bi, g: (bi, 0, 0)),
            pl.BlockSpec((1, META_PAD, KV_WIDTH), lambda bi, g: (0, bi, 0)),
            pl.BlockSpec((1, HEAD_DIM, SEQ), lambda bi, g: (bi, g, 0)),
            pl.BlockSpec((1, HEAD_DIM, META_PAD), lambda bi, g: (0, g, bi)),
        ],
        out_specs=pl.BlockSpec((1, n_qblk, gw, tq), q_map),
        out_shape=jax.ShapeDtypeStruct(qt.shape, bf16),
        scratch_shapes=[
            pltpu.VMEM((n_keys, tq), f32),
            pltpu.VMEM((HEAD_DIM + GQA_ONES_ROWS, n_keys), bf16),
        ],
        compiler_params=_params(("parallel", "parallel")),
        name="gqa_meta" if meta_queries else "gqa_attn",
    )(qt, k_real, k_meta, vt_real, vt_meta)


def _na_block_geometry(blk):
    r0 = blk * NA_QROWS
    kb = min(max(r0 - NA_KH // 2, 0), GRID_ROWS - NA_BAND_ROWS)
    typ = 0 if blk == 0 else (2 if blk == NA_NBLK - 1 else 1)
    return r0, kb, typ


def _na_row_window(r):
    rs = min(max(r - NA_KH // 2, 0), GRID_ROWS - NA_KH)
    return rs, rs + NA_KH


def _na_build_tables(rpb_ref, h, t_ref, tab_ref):
    kc = lax.broadcasted_iota(jnp.int32, (GRID_W, 128), 0)
    lane = lax.broadcasted_iota(jnp.int32, (GRID_W, 128), 1)
    c = lane % GRID_W
    rel = kc - c + (NA_KW - 1)
    cs = jnp.clip(c - NA_KW // 2, 0, GRID_W - NA_KW)
    cvalid = (kc >= cs) & (kc < cs + NA_KW)
    neg_tile = jnp.full((GRID_W, 128), NEG, f32)

    def build_a(a, carry):
        acc = neg_tile
        for r in range(2 * NA_KW - 1):
            acc = jnp.where(rel == r, rpb_ref[h, a, r] * LOG2E, acc)
        t_ref[a] = jnp.where(cvalid, acc, NEG)
        return carry

    lax.fori_loop(0, 2 * NA_KH - 1, build_a, 0)
    for typ, blk in enumerate((0, 1, NA_NBLK - 1)):
        r0, kb, _ = _na_block_geometry(blk)
        for i in range(NA_BAND_ROWS):
            kr = kb + i
            for jj in range(NA_QROWS // 2):
                halves = []
                for r in (r0 + 2 * jj, r0 + 2 * jj + 1):
                    lo, hi = _na_row_window(r)
                    halves.append(t_ref[kr - r + NA_KH - 1] if lo <= kr < hi else neg_tile)
                tab_ref[typ, i * GRID_W:(i + 1) * GRID_W, jj * 128:(jj + 1) * 128] = jnp.where(
                    lane < GRID_W, halves[0], halves[1])


def _na_kernel(rpb_ref, q_ref, k_ref, km_ref, vt_ref, vtm_ref, o_ref,
               t_ref, tab_ref, s_ref, vx_ref):
    @pl.when(pl.program_id(1) == 0)
    def _():
        _na_build_tables(rpb_ref, pl.program_id(0), t_ref, tab_ref)

    par = pl.program_id(0) % 2
    row = lax.broadcasted_iota(jnp.int32, (2 * HEAD_DIM, NA_QBLK), 0)
    sel = (row >= HEAD_DIM) == (par == 1)

    vx_ref[0:HEAD_DIM, 0:META_PAD] = vtm_ref[0]
    vx_ref[0:HEAD_DIM, META_PAD:] = vt_ref[0]
    vx_ref[HEAD_DIM:, :] = jnp.ones((GQA_ONES_ROWS, META_PAD + SEQ), bf16)

    def band_start(blk):
        kb = jnp.clip(blk * NA_QROWS - NA_KH // 2, 0, GRID_ROWS - NA_BAND_ROWS)
        return pl.multiple_of(kb * GRID_W, 2 * GRID_W)

    def table_type(blk):
        return jnp.where(blk == 0, 0, jnp.where(blk == NA_NBLK - 1, 2, 1))

    def q_lanes(blk):
        return pl.ds(pl.multiple_of(blk * NA_QBLK, NA_QBLK), NA_QBLK)

    def load_qext(blk):
        q = q_ref[0, :, q_lanes(blk)]
        return jnp.where(sel, jnp.concatenate([q, q], axis=0), jnp.zeros((), bf16))

    def block_scores(blk, qext):
        sm = _nn(km_ref[0, 0, 0:N_META, :], qext)
        s = _nn(k_ref[0, 0, pl.ds(band_start(blk), NA_BAND), :], qext) + tab_ref[table_type(blk)]
        return sm, s

    def block_output(blk, pm, p):
        ks = pl.multiple_of(META_PAD + band_start(blk), META_PAD)
        acc = _nn(vx_ref[:, 0:N_META], pm) + _nn(vx_ref[:, pl.ds(ks, NA_BAND)], p)
        l = acc[HEAD_DIM:HEAD_DIM + 1]
        o_ref[0, :, q_lanes(blk)] = (acc[0:HEAD_DIM] / l).astype(bf16)
        return l

    def fast_stage(u, carry):
        l_min, l_max = carry
        pending = None
        for j in range(NA_UNROLL + 1):
            if j < NA_UNROLL:
                blk = u * NA_UNROLL + j
                nxt = (blk,) + block_scores(blk, load_qext(blk))
            if pending is not None:
                pblk, sm, s = pending
                l = block_output(pblk, jnp.exp2(sm).astype(bf16), jnp.exp2(s).astype(bf16))
                l_min, l_max = jnp.minimum(l_min, l), jnp.maximum(l_max, l)
            pending = nxt
        return l_min, l_max

    ones = jnp.full((1, NA_QBLK), 1.0, f32)
    l_min, l_max = lax.fori_loop(0, NA_NBLK // NA_UNROLL, fast_stage, (ones, ones))

    def exact_block(blk, carry):
        sm, s = block_scores(blk, load_qext(blk))
        s_ref[0:N_META, :] = sm
        s_ref[META_PAD:, :] = s
        m = jnp.max(jnp.maximum(_colmax8(sm), _colmax8(s)), axis=0, keepdims=True)
        block_output(blk, jnp.exp2(s_ref[0:N_META, :] - m).astype(bf16),
                     jnp.exp2(s_ref[META_PAD:, :] - m).astype(bf16))
        return carry

    @pl.when(jnp.logical_not(_denominators_trusted(l_min, l_max)))
    def _():
        lax.fori_loop(0, NA_NBLK, exact_block, 0)


def _na_call(qt, k_real, k_meta, vt_real, vt_meta, rpb):
    b = qt.shape[0]
    return pl.pallas_call(
        _na_kernel,
        grid=(NA_HEADS, b),
        in_specs=[
            pl.BlockSpec(memory_space=pltpu.SMEM),
            pl.BlockSpec((1, HEAD_DIM, SEQ), lambda h, bi: (bi, h, 0)),
            pl.BlockSpec((1, 1, SEQ, 128), lambda h, bi: (bi, h // 2, 0, 0)),
            pl.BlockSpec((1, 1, META_PAD, 128), lambda h, bi: (0, h // 2, bi, 0)),
            pl.BlockSpec((1, HEAD_DIM, SEQ), lambda h, bi: (bi, h, 0)),
            pl.BlockSpec((1, HEAD_DIM, META_PAD), lambda h, bi: (0, h, bi)),
        ],
        out_specs=pl.BlockSpec((1, HEAD_DIM, SEQ), lambda h, bi: (bi, h, 0)),
        out_shape=jax.ShapeDtypeStruct(qt.shape, bf16),
        scratch_shapes=[
            pltpu.VMEM((2 * NA_KH - 1, GRID_W, 128), f32),
            pltpu.VMEM((3, NA_BAND, NA_QBLK), f32),
            pltpu.VMEM((META_PAD + NA_BAND, NA_QBLK), f32),
            pltpu.VMEM((HEAD_DIM + GQA_ONES_ROWS, META_PAD + SEQ), bf16),
        ],
        compiler_params=_params(("arbitrary", "arbitrary")),
        name="na_attn",
    )(rpb, qt, k_real, k_meta, vt_real, vt_meta)


def _na_meta_kernel(q_ref, km_ref, vtm_ref, o_ref):
    row = lax.broadcasted_iota(jnp.int32, (2 * HEAD_DIM, META_PAD), 0)
    meta_valid = row < N_META
    for h in range(NA_HEADS):
        lo = h * HEAD_DIM
        q = q_ref[0, lo:lo + HEAD_DIM, :]
        sel = (row >= HEAD_DIM) == (h % 2 == 1)
        qext = jnp.where(sel, jnp.concatenate([q, q], axis=0), jnp.zeros((), bf16))
        s = jnp.where(meta_valid, _nn(km_ref[0, h // 2], qext), NEG)
        m = jnp.max(s, axis=0, keepdims=True)
        p = jnp.exp2(s - m)
        l = jnp.sum(p, axis=0, keepdims=True)
        acc = _nn(vtm_ref[0, lo:lo + HEAD_DIM, :], p.astype(bf16))
        o_ref[0, lo:lo + HEAD_DIM, :] = (acc / l).astype(bf16)


def _na_meta_call(qt_meta, k_meta, vt_meta, b):
    return pl.pallas_call(
        _na_meta_kernel,
        grid=(b,),
        in_specs=[
            pl.BlockSpec((1, NA_WIDTH, META_PAD), lambda bi: (0, 0, bi)),
            pl.BlockSpec((1, NA_HEADS // 2, META_PAD, 128), lambda bi: (0, 0, bi, 0)),
            pl.BlockSpec((1, NA_WIDTH, META_PAD), lambda bi: (0, 0, bi)),
        ],
        out_specs=pl.BlockSpec((1, NA_WIDTH, META_PAD), lambda bi: (0, 0, bi)),
        out_shape=jax.ShapeDtypeStruct(qt_meta.shape, bf16),
        compiler_params=_params(("parallel",)),
        name="na_meta",
    )(qt_meta, k_meta, vt_meta)


def _out_kernel(x_ref, g_ref, ata_ref, atb_ref, w2_ref, woa_ref, wob_ref, wout_ref, fg_ref,
                o_ref, *, final):
    x = x_ref[0]
    h = _rms_rows(x, g_ref[...]).astype(bf16)

    def gated(at_ref, z):
        return (at_ref[...].reshape(z.shape).astype(f32) * (z * _sigmoid(z))).astype(bf16)

    za = _nt(w2_ref[P2_ZA:P2_ZB, :], h)
    zb = _nt(w2_ref[P2_ZB:P2_GA, :], h)
    ga = _nt(w2_ref[P2_GA:P2_GB, :], h)
    gb = _nt(w2_ref[P2_GB:P2_END, :], h)
    ya = _nn(woa_ref[...], gated(ata_ref, za))
    yb = _nn(wob_ref[...], gated(atb_ref, zb))
    mix = (_sigmoid(ga) * ya + _sigmoid(gb) * yb).astype(bf16)
    out = x + _tn(mix, wout_ref[...])
    if final:
        out = _rms_rows(out, fg_ref[...])
    o_ref[0] = out


def _out_call(x3, norm_g, at_a, at_b, w2t, woat, wobt, wout, final_g, tm, final):
    bx, lx, _ = x3.shape
    const = lambda b, j: (0, 0)
    return pl.pallas_call(
        functools.partial(_out_kernel, final=final),
        grid=(bx, lx // tm),
        in_specs=[
            pl.BlockSpec((1, tm, D_MODEL), lambda b, j: (b, j, 0)),
            pl.BlockSpec((1, D_MODEL), const),
            pl.BlockSpec((1, 1, ATTN_WIDTH, tm), lambda b, j: (b, j, 0, 0)),
            pl.BlockSpec((1, NA_WIDTH, tm), lambda b, j: (b, 0, j)),
            pl.BlockSpec((P2_END, D_MODEL), const),
            pl.BlockSpec((D_MODEL, ATTN_WIDTH), const),
            pl.BlockSpec((D_MODEL, NA_WIDTH), const),
            pl.BlockSpec((D_MODEL, D_MODEL), const),
            pl.BlockSpec((1, D_MODEL), const),
        ],
        out_specs=pl.BlockSpec((1, tm, D_MODEL), lambda b, j: (b, j, 0)),
        out_shape=jax.ShapeDtypeStruct(x3.shape, f32),
        compiler_params=_params(("parallel", "parallel")),
        name="gate_out_proj",
    )(x3, norm_g, at_a, at_b, w2t, woat, wobt, wout, final_g)


def _rope_tables():
    t = np.arange(SEQ)
    axis_dim = HEAD_DIM // 2
    inv = jnp.asarray(ROPE_THETA, f32) ** (-jnp.arange(0, axis_dim, 2, dtype=f32) / axis_dim)
    ang_r = jnp.asarray(t // GRID_W, f32)[None, :] * inv[:, None]
    ang_c = jnp.asarray(t % GRID_W, f32)[None, :] * inv[:, None]
    ang = jnp.concatenate([ang_r, ang_r, ang_c, ang_c], axis=0)
    sign = np.repeat(np.array([-1.0, 1.0, -1.0, 1.0], np.float32), axis_dim // 2)[:, None]
    return jnp.cos(ang), jnp.sin(ang) * sign


def kernel(x, meta_tokens, norm_g, w_in, q_norm_g, k_norm_g, na_rpb, w_o_attn, w_o_na, w_out,
           final_norm_g):
    b = x.shape[0]
    depth = w_in.shape[0]
    cos_r, sin_r = _rope_tables()
    cos_m = jnp.ones((HEAD_DIM, b * META_PAD), f32)
    sin_m = jnp.zeros((HEAD_DIM, b * META_PAD), f32)

    meta_pad = jnp.concatenate(
        [meta_tokens.astype(f32), jnp.zeros((META_PAD - N_META, D_MODEL), f32)], axis=0)
    xm = jnp.tile(meta_pad, (b, 1))[None]
    xr = x
    fg = final_norm_g.reshape(1, D_MODEL)

    for i in range(depth):
        wi = w_in[i]
        qa, ka, va, za, qb, kb, vb, zb, ga, gb = (
            wi[:, 0:512], wi[:, 512:640], wi[:, 640:768], wi[:, 768:1280], wi[:, 1280:1792],
            wi[:, 1792:2304], wi[:, 2304:2816], wi[:, 2816:3328], wi[:, 3328:4352], wi[:, 4352:5376])
        w1t = jnp.concatenate([qa, ka, va, qb, kb, vb], axis=1).T.astype(bf16)
        w2t = jnp.concatenate([za, zb, ga, gb], axis=1).T.astype(bf16)
        woat = w_o_attn[i].T.astype(bf16)
        wobt = w_o_na[i].T.astype(bf16)
        wout = w_out[i].astype(bf16)
        ng = norm_g[i].reshape(1, D_MODEL)
        qg = q_norm_g[i].reshape(HEAD_DIM, 1)
        kg = k_norm_g[i].reshape(HEAD_DIM, 1)
        last = i == depth - 1

        qta, k_a, vta, qtb, k_b, vtb = _proj_call(xr, ng, w1t, qg, kg, cos_r, sin_r, TOKEN_TILE)
        qta_m, k_a_m, vta_m, qtb_m, k_b_m, vtb_m = _proj_call(
            xm, ng, w1t, qg, kg, cos_m, sin_m, TOKEN_TILE)

        at_a = _gqa_call(qta, k_a, k_a_m, vta, vta_m, meta_queries=False)
        at_b = _na_call(qtb, k_b, k_b_m, vtb, vtb_m, na_rpb[i])
        xr_new = _out_call(xr, ng, at_a, at_b, w2t, woat, wobt, wout, fg, TOKEN_TILE, last)
        if not last:
            at_a_m = _unpack_meta_outputs(
                _gqa_call(_pack_meta_queries(qta_m, b), k_a, k_a_m, vta, vta_m, meta_queries=True),
                qta_m)
            at_b_m = _na_meta_call(qtb_m, k_b_m, vtb_m, b)
            xm = _out_call(xm, ng, at_a_m, at_b_m, w2t, woat, wobt, wout, fg, TOKEN_TILE, False)
        xr = xr_new
    return xr
```

```python
import functools

import numpy as np
import jax
import jax.numpy as jnp
from jax import lax
from jax.experimental import pallas as pl
from jax.experimental.pallas import tpu as pltpu

D_MODEL = 1024
SEQ = 4096
GRID_W = 64
GRID_ROWS = SEQ // GRID_W
N_META = 16
META_PAD = 128
HEAD_DIM = 64
ATTN_HEADS = 8
ATTN_KV_HEADS = 2
ATTN_GROUP = ATTN_HEADS // ATTN_KV_HEADS
NA_HEADS = 8
NA_KH = 8
NA_KW = 16
ROPE_THETA = 10000.0
EPS = 1e-6
SCALE = HEAD_DIM ** -0.5
LOG2E = 1.4426950408889634
Q_SCALE = SCALE * LOG2E
NEG = -1e30

ATTN_WIDTH = ATTN_HEADS * HEAD_DIM
KV_WIDTH = ATTN_KV_HEADS * HEAD_DIM
NA_WIDTH = NA_HEADS * HEAD_DIM

P1_QA, P1_KA, P1_VA, P1_QB, P1_KB, P1_VB, P1_END = 0, 512, 640, 768, 1280, 1792, 2304
P2_ZA, P2_ZB, P2_GA, P2_GB, P2_END = 0, 512, 1024, 2048, 3072

NA_QROWS = 4
NA_QBLK = NA_QROWS * GRID_W
NA_BAND_ROWS = 12
NA_BAND = NA_BAND_ROWS * GRID_W
NA_NBLK = GRID_ROWS // NA_QROWS
NA_UNROLL = 16

TOKEN_TILE = 512
GQA_TK = 512
GQA_UNROLL = 4
DENOM_FLOOR = 2.0 ** -64
DENOM_CEIL = 2.0 ** 100
VMEM_LIMIT = 56 * 1024 * 1024

bf16 = jnp.bfloat16
f32 = jnp.float32


def _nn(a, b):
    return lax.dot_general(a, b, (((1,), (0,)), ((), ())), preferred_element_type=f32)


def _nt(a, b):
    return lax.dot_general(a, b, (((1,), (1,)), ((), ())), preferred_element_type=f32)


def _tn(a, b):
    return lax.dot_general(a, b, (((0,), (0,)), ((), ())), preferred_element_type=f32)


def _rms_rows(x, g):
    return x * lax.rsqrt(jnp.mean(x * x, axis=-1, keepdims=True) + EPS) * g


def _sigmoid(x):
    return 1.0 / (1.0 + jnp.exp(-x))


def _params(sem):
    return pltpu.CompilerParams(dimension_semantics=sem, vmem_limit_bytes=VMEM_LIMIT)


def _proj_kernel(x_ref, g_ref, w_ref, qg_ref, kg_ref, cos_ref, sin_ref,
                 qta_ref, ka_ref, vta_ref, qtb_ref, kb_ref, vtb_ref):
    h = _rms_rows(x_ref[0], g_ref[...]).astype(bf16)
    cos = cos_ref[...]
    sin = sin_ref[...]

    def norm_rope(t, gain):
        y = t * lax.rsqrt(jnp.mean(t * t, axis=0, keepdims=True) + EPS) * gain
        sw = jnp.concatenate([y[16:32], y[0:16], y[48:64], y[32:48]], axis=0)
        return y * cos + sw * sin

    qg = qg_ref[...]
    kg = kg_ref[...]
    def proj(lo, hi):
        return _nt(w_ref[lo:hi, :], h)

    pa = proj(P1_QA, P1_QB)
    qa, ka, va = pa[P1_QA:P1_KA], pa[P1_KA:P1_VA], pa[P1_VA:P1_QB]
    qb = proj(P1_QB, P1_KB)
    kbt = proj(P1_KB, P1_VB)
    vb = proj(P1_VB, P1_END)
    for hd in range(ATTN_HEADS):
        lo = hd * HEAD_DIM
        qta_ref[0, 0, lo:lo + HEAD_DIM, :] = (norm_rope(qa[lo:lo + HEAD_DIM], qg) * Q_SCALE).astype(bf16)
    kt = jnp.concatenate([norm_rope(ka[0:HEAD_DIM], kg), norm_rope(ka[HEAD_DIM:], kg)], axis=0)
    ka_ref[0] = kt.T.astype(bf16)
    vta_ref[0] = va.astype(bf16)
    qtb_ref[0] = (qb * Q_SCALE).astype(bf16)
    for p in range(NA_HEADS // 2):
        kb_ref[0, p] = kbt[128 * p:128 * (p + 1)].T.astype(bf16)
    vtb_ref[0] = vb.astype(bf16)


def _proj_call(x3, norm_g, w1t, qg, kg, cos_t, sin_t, tm):
    bx, lx, _ = x3.shape
    grid = (bx, lx // tm)
    out_shape = (
        jax.ShapeDtypeStruct((bx, lx // tm, ATTN_WIDTH, tm), bf16),
        jax.ShapeDtypeStruct((bx, lx, KV_WIDTH), bf16),
        jax.ShapeDtypeStruct((bx, KV_WIDTH, lx), bf16),
        jax.ShapeDtypeStruct((bx, NA_WIDTH, lx), bf16),
        jax.ShapeDtypeStruct((bx, NA_HEADS // 2, lx, 128), bf16),
        jax.ShapeDtypeStruct((bx, NA_WIDTH, lx), bf16),
    )
    return pl.pallas_call(
        _proj_kernel,
        grid=grid,
        in_specs=[
            pl.BlockSpec((1, tm, D_MODEL), lambda b, j: (b, j, 0)),
            pl.BlockSpec((1, D_MODEL), lambda b, j: (0, 0)),
            pl.BlockSpec((P1_END, D_MODEL), lambda b, j: (0, 0)),
            pl.BlockSpec((HEAD_DIM, 1), lambda b, j: (0, 0)),
            pl.BlockSpec((HEAD_DIM, 1), lambda b, j: (0, 0)),
            pl.BlockSpec((HEAD_DIM, tm), lambda b, j: (0, j)),
            pl.BlockSpec((HEAD_DIM, tm), lambda b, j: (0, j)),
        ],
        out_specs=(
            pl.BlockSpec((1, 1, ATTN_WIDTH, tm), lambda b, j: (b, j, 0, 0)),
            pl.BlockSpec((1, tm, KV_WIDTH), lambda b, j: (b, j, 0)),
            pl.BlockSpec((1, KV_WIDTH, tm), lambda b, j: (b, 0, j)),
            pl.BlockSpec((1, NA_WIDTH, tm), lambda b, j: (b, 0, j)),
            pl.BlockSpec((1, NA_HEADS // 2, tm, 128), lambda b, j: (b, 0, j, 0)),
            pl.BlockSpec((1, NA_WIDTH, tm), lambda b, j: (b, 0, j)),
        ),
        out_shape=out_shape,
        compiler_params=_params(("parallel", "parallel")),
        name="qkv_proj",
    )(x3, norm_g, w1t, qg, kg, cos_t, sin_t)


def _colmax8(x):
    return jnp.max(x.reshape(x.shape[0] // 8, 8, x.shape[1]), axis=0)


def _colsum8(x):
    return jnp.sum(x.reshape(x.shape[0] // 8, 8, x.shape[1]), axis=0)


def _denominators_trusted(l_min, l_max):
    return jnp.logical_and(jnp.min(l_min) >= DENOM_FLOOR, jnp.max(l_max) <= DENOM_CEIL)


def _gqa_kernel(q_ref, k_ref, km_ref, vt_ref, vtm_ref, o_ref, s_ref, *, tq, n_qblk, heads):
    g = pl.program_id(1)
    row = lax.broadcasted_iota(jnp.int32, (2 * HEAD_DIM, tq), 0)
    sel = (row >= HEAD_DIM) == (g == 1)
    n_items = n_qblk * heads
    n_chunks = SEQ // GQA_TK
    blocks = [(0, N_META)] + [(META_PAD + c * GQA_TK, GQA_TK) for c in range(n_chunks)]

    def item_index(t):
        return t // heads, pl.multiple_of((t % heads) * HEAD_DIM, HEAD_DIM)

    def load_qext(t):
        qi, off = item_index(t)
        q = q_ref[0, qi, pl.ds(off, HEAD_DIM), :]
        return jnp.where(sel, jnp.concatenate([q, q], axis=0), jnp.zeros((), bf16))

    def score_block(qext, lo, n):
        if lo == 0:
            return _nn(km_ref[0, 0:n, :], qext)
        return _nn(k_ref[0, lo - META_PAD:lo - META_PAD + n, :], qext)

    def pv_block(p, lo, n):
        if lo == 0:
            return _nn(vtm_ref[0, :, 0:n], p.astype(bf16))
        return _nn(vt_ref[0, :, lo - META_PAD:lo - META_PAD + n], p.astype(bf16))

    def accumulate(acc, l8, p, lo, n):
        d = pv_block(p, lo, n)
        return (d, _colsum8(p)) if acc is None else (acc + d, l8 + _colsum8(p))

    def store_item(t, acc, l8):
        l = jnp.sum(l8, axis=0, keepdims=True)
        qi, off = item_index(t)
        o_ref[0, qi, pl.ds(off, HEAD_DIM), :] = (acc / l).astype(bf16)
        return l

    def fast_item(t, carry):
        l_min, l_max = carry
        qext = load_qext(t)
        acc, l8, pending = None, None, None
        for lo, n in blocks + [(None, None)]:
            s = score_block(qext, lo, n) if lo is not None else None
            if pending is not None:
                s_prev, plo, pn = pending
                acc, l8 = accumulate(acc, l8, jnp.exp2(s_prev), plo, pn)
            pending = (s, lo, n)
        l = store_item(t, acc, l8)
        return jnp.minimum(l_min, l), jnp.maximum(l_max, l)

    ones = jnp.full((1, tq), 1.0, f32)
    l_min, l_max = lax.fori_loop(0, n_items, fast_item, (ones, ones), unroll=min(GQA_UNROLL, n_items))

    def exact_item(t, carry):
        qext = load_qext(t)
        m8 = None
        for lo, n in blocks:
            s = score_block(qext, lo, n)
            s_ref[lo:lo + n, :] = s
            m8 = _colmax8(s) if m8 is None else jnp.maximum(m8, _colmax8(s))
        m = jnp.max(m8, axis=0, keepdims=True)
        acc, l8 = None, None
        for lo, n in blocks:
            acc, l8 = accumulate(acc, l8, jnp.exp2(s_ref[lo:lo + n, :] - m), lo, n)
        store_item(t, acc, l8)
        return carry

    @pl.when(jnp.logical_not(_denominators_trusted(l_min, l_max)))
    def _():
        lax.fori_loop(0, n_items, exact_item, 0)


def _pack_meta_queries(qt_meta, b):
    q = qt_meta.reshape(-1, ATTN_WIDTH, TOKEN_TILE // META_PAD, META_PAD)
    q = q.transpose(0, 2, 1, 3).reshape(b, ATTN_KV_HEADS, ATTN_GROUP, HEAD_DIM, META_PAD)
    q = q[..., :N_META].transpose(0, 1, 3, 2, 4).reshape(b, KV_WIDTH, ATTN_GROUP * N_META)
    return jnp.pad(q, ((0, 0), (0, 0), (0, META_PAD - ATTN_GROUP * N_META)))[:, None]


def _unpack_meta_outputs(o_packed, like):
    b = o_packed.shape[0]
    o = o_packed[:, 0, :, :ATTN_GROUP * N_META].reshape(b, ATTN_KV_HEADS, HEAD_DIM, ATTN_GROUP, N_META)
    o = o.transpose(0, 1, 3, 2, 4).reshape(b, ATTN_WIDTH, N_META)
    o = jnp.pad(o, ((0, 0), (0, 0), (0, META_PAD - N_META)))
    per_tile = TOKEN_TILE // META_PAD
    o = o.reshape(b // per_tile, per_tile, ATTN_WIDTH, META_PAD).transpose(0, 2, 1, 3)
    return o.reshape(like.shape)


def _gqa_call(qt, k_real, k_meta, vt_real, vt_meta, *, meta_queries):
    b = k_real.shape[0]
    if meta_queries:
        tq, n_qblk, heads = META_PAD, 1, 1
    else:
        tq, n_qblk, heads = TOKEN_TILE, SEQ // TOKEN_TILE, ATTN_GROUP
    gw = heads * HEAD_DIM
    q_map = lambda bi, g: (bi, 0, g, 0)
    n_keys = META_PAD + SEQ
    return pl.pallas_call(
        functools.partial(_gqa_kernel, tq=tq, n_qblk=n_qblk, heads=heads),
        grid=(b, ATTN_KV_HEADS),
        in_specs=[
            pl.BlockSpec((1, n_qblk, gw, tq), q_map),
            pl.BlockSpec((1, SEQ, KV_WIDTH), lambda bi, g: (bi, 0, 0)),
            pl.BlockSpec((1, META_PAD, KV_WIDTH), lambda bi, g: (0, bi, 0)),
            pl.BlockSpec((1, HEAD_DIM, SEQ), lambda bi, g: (bi, g, 0)),
            pl.BlockSpec((1, HEAD_DIM, META_PAD), lambda bi, g: (0, g, bi)),
        ],
        out_specs=pl.BlockSpec((1, n_qblk, gw, tq), q_map),
        out_shape=jax.ShapeDtypeStruct(qt.shape, bf16),
        scratch_shapes=[pltpu.VMEM((n_keys, tq), f32)],
        compiler_params=_params(("parallel", "parallel")),
        name="gqa_meta" if meta_queries else "gqa_attn",
    )(qt, k_real, k_meta, vt_real, vt_meta)


def _na_block_geometry(blk):
    r0 = blk * NA_QROWS
    kb = min(max(r0 - NA_KH // 2, 0), GRID_ROWS - NA_BAND_ROWS)
    typ = 0 if blk == 0 else (2 if blk == NA_NBLK - 1 else 1)
    return r0, kb, typ


def _na_row_window(r):
    rs = min(max(r - NA_KH // 2, 0), GRID_ROWS - NA_KH)
    return rs, rs + NA_KH


def _na_build_tables(rpb_ref, h, t_ref, tab_ref):
    kc = lax.broadcasted_iota(jnp.int32, (GRID_W, 128), 0)
    lane = lax.broadcasted_iota(jnp.int32, (GRID_W, 128), 1)
    c = lane % GRID_W
    rel = kc - c + (NA_KW - 1)
    cs = jnp.clip(c - NA_KW // 2, 0, GRID_W - NA_KW)
    cvalid = (kc >= cs) & (kc < cs + NA_KW)
    neg_tile = jnp.full((GRID_W, 128), NEG, f32)

    def build_a(a, carry):
        acc = neg_tile
        for r in range(2 * NA_KW - 1):
            acc = jnp.where(rel == r, rpb_ref[h, a, r] * LOG2E, acc)
        t_ref[a] = jnp.where(cvalid, acc, NEG)
        return carry

    lax.fori_loop(0, 2 * NA_KH - 1, build_a, 0)
    for typ, blk in enumerate((0, 1, NA_NBLK - 1)):
        r0, kb, _ = _na_block_geometry(blk)
        for i in range(NA_BAND_ROWS):
            kr = kb + i
            for jj in range(NA_QROWS // 2):
                halves = []
                for r in (r0 + 2 * jj, r0 + 2 * jj + 1):
                    lo, hi = _na_row_window(r)
                    halves.append(t_ref[kr - r + NA_KH - 1] if lo <= kr < hi else neg_tile)
                tab_ref[typ, i * GRID_W:(i + 1) * GRID_W, jj * 128:(jj + 1) * 128] = jnp.where(
                    lane < GRID_W, halves[0], halves[1])


def _na_kernel(rpb_ref, q_ref, k_ref, km_ref, vt_ref, vtm_ref, o_ref,
               t_ref, tab_ref, s_ref):
    @pl.when(pl.program_id(1) == 0)
    def _():
        _na_build_tables(rpb_ref, pl.program_id(0), t_ref, tab_ref)

    par = pl.program_id(0) % 2
    row = lax.broadcasted_iota(jnp.int32, (2 * HEAD_DIM, NA_QBLK), 0)
    sel = (row >= HEAD_DIM) == (par == 1)

    def band_start(blk):
        kb = jnp.clip(blk * NA_QROWS - NA_KH // 2, 0, GRID_ROWS - NA_BAND_ROWS)
        return pl.multiple_of(kb * GRID_W, 2 * GRID_W)

    def table_type(blk):
        return jnp.where(blk == 0, 0, jnp.where(blk == NA_NBLK - 1, 2, 1))

    def q_lanes(blk):
        return pl.ds(pl.multiple_of(blk * NA_QBLK, NA_QBLK), NA_QBLK)

    def load_qext(blk):
        q = q_ref[0, :, q_lanes(blk)]
        return jnp.where(sel, jnp.concatenate([q, q], axis=0), jnp.zeros((), bf16))

    def block_scores(blk, qext):
        sm = _nn(km_ref[0, 0, 0:N_META, :], qext)
        s = _nn(k_ref[0, 0, pl.ds(band_start(blk), NA_BAND), :], qext) + tab_ref[table_type(blk)]
        return sm, s

    def block_output(blk, pm, p):
        acc = (_nn(vtm_ref[0, :, 0:N_META], pm.astype(bf16))
               + _nn(vt_ref[0, :, pl.ds(band_start(blk), NA_BAND)], p.astype(bf16)))
        l = jnp.sum(_colsum8(pm) + _colsum8(p), axis=0, keepdims=True)
        o_ref[0, :, q_lanes(blk)] = (acc / l).astype(bf16)
        return l

    def fast_stage(u, carry):
        l_min, l_max = carry
        pending = None
        for j in range(NA_UNROLL + 1):
            if j < NA_UNROLL:
                blk = u * NA_UNROLL + j
                nxt = (blk,) + block_scores(blk, load_qext(blk))
            if pending is not None:
                pblk, sm, s = pending
                l = block_output(pblk, jnp.exp2(sm), jnp.exp2(s))
                l_min, l_max = jnp.minimum(l_min, l), jnp.maximum(l_max, l)
            pending = nxt
        return l_min, l_max

    ones = jnp.full((1, NA_QBLK), 1.0, f32)
    l_min, l_max = lax.fori_loop(0, NA_NBLK // NA_UNROLL, fast_stage, (ones, ones))

    def exact_block(blk, carry):
        sm, s = block_scores(blk, load_qext(blk))
        s_ref[0:N_META, :] = sm
        s_ref[META_PAD:, :] = s
        m = jnp.max(jnp.maximum(_colmax8(sm), _colmax8(s)), axis=0, keepdims=True)
        block_output(blk, jnp.exp2(s_ref[0:N_META, :] - m), jnp.exp2(s_ref[META_PAD:, :] - m))
        return carry

    @pl.when(jnp.logical_not(_denominators_trusted(l_min, l_max)))
    def _():
        lax.fori_loop(0, NA_NBLK, exact_block, 0)


def _na_call(qt, k_real, k_meta, vt_real, vt_meta, rpb):
    b = qt.shape[0]
    return pl.pallas_call(
        _na_kernel,
        grid=(NA_HEADS, b),
        in_specs=[
            pl.BlockSpec(memory_space=pltpu.SMEM),
            pl.BlockSpec((1, HEAD_DIM, SEQ), lambda h, bi: (bi, h, 0)),
            pl.BlockSpec((1, 1, SEQ, 128), lambda h, bi: (bi, h // 2, 0, 0)),
            pl.BlockSpec((1, 1, META_PAD, 128), lambda h, bi: (0, h // 2, bi, 0)),
            pl.BlockSpec((1, HEAD_DIM, SEQ), lambda h, bi: (bi, h, 0)),
            pl.BlockSpec((1, HEAD_DIM, META_PAD), lambda h, bi: (0, h, bi)),
        ],
        out_specs=pl.BlockSpec((1, HEAD_DIM, SEQ), lambda h, bi: (bi, h, 0)),
        out_shape=jax.ShapeDtypeStruct(qt.shape, bf16),
        scratch_shapes=[
            pltpu.VMEM((2 * NA_KH - 1, GRID_W, 128), f32),
            pltpu.VMEM((3, NA_BAND, NA_QBLK), f32),
            pltpu.VMEM((META_PAD + NA_BAND, NA_QBLK), f32),
        ],
        compiler_params=_params(("arbitrary", "arbitrary")),
        name="na_attn",
    )(rpb, qt, k_real, k_meta, vt_real, vt_meta)


def _na_meta_kernel(q_ref, km_ref, vtm_ref, o_ref):
    row = lax.broadcasted_iota(jnp.int32, (2 * HEAD_DIM, META_PAD), 0)
    meta_valid = row < N_META
    for h in range(NA_HEADS):
        lo = h * HEAD_DIM
        q = q_ref[0, lo:lo + HEAD_DIM, :]
        sel = (row >= HEAD_DIM) == (h % 2 == 1)
        qext = jnp.where(sel, jnp.concatenate([q, q], axis=0), jnp.zeros((), bf16))
        s = jnp.where(meta_valid, _nn(km_ref[0, h // 2], qext), NEG)
        m = jnp.max(s, axis=0, keepdims=True)
        p = jnp.exp2(s - m)
        l = jnp.sum(p, axis=0, keepdims=True)
        acc = _nn(vtm_ref[0, lo:lo + HEAD_DIM, :], p.astype(bf16))
        o_ref[0, lo:lo + HEAD_DIM, :] = (acc / l).astype(bf16)


def _na_meta_call(qt_meta, k_meta, vt_meta, b):
    return pl.pallas_call(
        _na_meta_kernel,
        grid=(b,),
        in_specs=[
            pl.BlockSpec((1, NA_WIDTH, META_PAD), lambda bi: (0, 0, bi)),
            pl.BlockSpec((1, NA_HEADS // 2, META_PAD, 128), lambda bi: (0, 0, bi, 0)),
            pl.BlockSpec((1, NA_WIDTH, META_PAD), lambda bi: (0, 0, bi)),
        ],
        out_specs=pl.BlockSpec((1, NA_WIDTH, META_PAD), lambda bi: (0, 0, bi)),
        out_shape=jax.ShapeDtypeStruct(qt_meta.shape, bf16),
        compiler_params=_params(("parallel",)),
        name="na_meta",
    )(qt_meta, k_meta, vt_meta)


def _out_kernel(x_ref, g_ref, ata_ref, atb_ref, w2_ref, woa_ref, wob_ref, wout_ref, fg_ref,
                o_ref, *, final):
    x = x_ref[0]
    h = _rms_rows(x, g_ref[...]).astype(bf16)

    def gated(at_ref, z):
        return (at_ref[...].reshape(z.shape).astype(f32) * (z * _sigmoid(z))).astype(bf16)

    za = _nt(w2_ref[P2_ZA:P2_ZB, :], h)
    zb = _nt(w2_ref[P2_ZB:P2_GA, :], h)
    ga = _nt(w2_ref[P2_GA:P2_GB, :], h)
    gb = _nt(w2_ref[P2_GB:P2_END, :], h)
    ya = _nn(woa_ref[...], gated(ata_ref, za))
    yb = _nn(wob_ref[...], gated(atb_ref, zb))
    mix = (_sigmoid(ga) * ya + _sigmoid(gb) * yb).astype(bf16)
    out = x + _tn(mix, wout_ref[...])
    if final:
        out = _rms_rows(out, fg_ref[...])
    o_ref[0] = out


def _out_call(x3, norm_g, at_a, at_b, w2t, woat, wobt, wout, final_g, tm, final):
    bx, lx, _ = x3.shape
    const = lambda b, j: (0, 0)
    return pl.pallas_call(
        functools.partial(_out_kernel, final=final),
        grid=(bx, lx // tm),
        in_specs=[
            pl.BlockSpec((1, tm, D_MODEL), lambda b, j: (b, j, 0)),
            pl.BlockSpec((1, D_MODEL), const),
            pl.BlockSpec((1, 1, ATTN_WIDTH, tm), lambda b, j: (b, j, 0, 0)),
            pl.BlockSpec((1, NA_WIDTH, tm), lambda b, j: (b, 0, j)),
            pl.BlockSpec((P2_END, D_MODEL), const),
            pl.BlockSpec((D_MODEL, ATTN_WIDTH), const),
            pl.BlockSpec((D_MODEL, NA_WIDTH), const),
            pl.BlockSpec((D_MODEL, D_MODEL), const),
            pl.BlockSpec((1, D_MODEL), const),
        ],
        out_specs=pl.BlockSpec((1, tm, D_MODEL), lambda b, j: (b, j, 0)),
        out_shape=jax.ShapeDtypeStruct(x3.shape, f32),
        compiler_params=_params(("parallel", "parallel")),
        name="gate_out_proj",
    )(x3, norm_g, at_a, at_b, w2t, woat, wobt, wout, final_g)


def _rope_tables():
    t = np.arange(SEQ)
    axis_dim = HEAD_DIM // 2
    inv = jnp.asarray(ROPE_THETA, f32) ** (-jnp.arange(0, axis_dim, 2, dtype=f32) / axis_dim)
    ang_r = jnp.asarray(t // GRID_W, f32)[None, :] * inv[:, None]
    ang_c = jnp.asarray(t % GRID_W, f32)[None, :] * inv[:, None]
    ang = jnp.concatenate([ang_r, ang_r, ang_c, ang_c], axis=0)
    sign = np.repeat(np.array([-1.0, 1.0, -1.0, 1.0], np.float32), axis_dim // 2)[:, None]
    return jnp.cos(ang), jnp.sin(ang) * sign


def kernel(x, meta_tokens, norm_g, w_in, q_norm_g, k_norm_g, na_rpb, w_o_attn, w_o_na, w_out,
           final_norm_g):
    b = x.shape[0]
    depth = w_in.shape[0]
    cos_r, sin_r = _rope_tables()
    cos_m = jnp.ones((HEAD_DIM, b * META_PAD), f32)
    sin_m = jnp.zeros((HEAD_DIM, b * META_PAD), f32)

    meta_pad = jnp.concatenate(
        [meta_tokens.astype(f32), jnp.zeros((META_PAD - N_META, D_MODEL), f32)], axis=0)
    xm = jnp.tile(meta_pad, (b, 1))[None]
    xr = x
    fg = final_norm_g.reshape(1, D_MODEL)

    for i in range(depth):
        wi = w_in[i]
        qa, ka, va, za, qb, kb, vb, zb, ga, gb = (
            wi[:, 0:512], wi[:, 512:640], wi[:, 640:768], wi[:, 768:1280], wi[:, 1280:1792],
            wi[:, 1792:2304], wi[:, 2304:2816], wi[:, 2816:3328], wi[:, 3328:4352], wi[:, 4352:5376])
        w1t = jnp.concatenate([qa, ka, va, qb, kb, vb], axis=1).T.astype(bf16)
        w2t = jnp.concatenate([za, zb, ga, gb], axis=1).T.astype(bf16)
        woat = w_o_attn[i].T.astype(bf16)
        wobt = w_o_na[i].T.astype(bf16)
        wout = w_out[i].astype(bf16)
        ng = norm_g[i].reshape(1, D_MODEL)
        qg = q_norm_g[i].reshape(HEAD_DIM, 1)
        kg = k_norm_g[i].reshape(HEAD_DIM, 1)
        last = i == depth - 1

        qta, k_a, vta, qtb, k_b, vtb = _proj_call(xr, ng, w1t, qg, kg, cos_r, sin_r, TOKEN_TILE)
        qta_m, k_a_m, vta_m, qtb_m, k_b_m, vtb_m = _proj_call(
            xm, ng, w1t, qg, kg, cos_m, sin_m, TOKEN_TILE)

        at_a = _gqa_call(qta, k_a, k_a_m, vta, vta_m, meta_queries=False)
        at_b = _na_call(qtb, k_b, k_b_m, vtb, vtb_m, na_rpb[i])
        xr_new = _out_call(xr, ng, at_a, at_b, w2t, woat, wobt, wout, fg, TOKEN_TILE, last)
        if not last:
            at_a_m = _unpack_meta_outputs(
                _gqa_call(_pack_meta_queries(qta_m, b), k_a, k_a_m, vta, vta_m, meta_queries=True),
                qta_m)
            at_b_m = _na_meta_call(qtb_m, k_b_m, vtb_m, b)
            xm = _out_call(xm, ng, at_a_m, at_b_m, w2t, woat, wobt, wout, fg, TOKEN_TILE, False)
        xr = xr_new
    return xr
```

```python
import functools

import numpy as np
import jax
import jax.numpy as jnp
from jax import lax
from jax.experimental import pallas as pl
from jax.experimental.pallas import tpu as pltpu

D_MODEL = 1024
SEQ = 4096
GRID_W = 64
GRID_ROWS = SEQ // GRID_W
N_META = 16
META_PAD = 128
HEAD_DIM = 64
ATTN_HEADS = 8
ATTN_KV_HEADS = 2
ATTN_GROUP = ATTN_HEADS // ATTN_KV_HEADS
NA_HEADS = 8
NA_KH = 8
NA_KW = 16
ROPE_THETA = 10000.0
EPS = 1e-6
SCALE = HEAD_DIM ** -0.5
LOG2E = 1.4426950408889634
Q_SCALE = SCALE * LOG2E
NEG = -1e30

ATTN_WIDTH = ATTN_HEADS * HEAD_DIM
KV_WIDTH = ATTN_KV_HEADS * HEAD_DIM
NA_WIDTH = NA_HEADS * HEAD_DIM

P1_QA, P1_KA, P1_VA, P1_QB, P1_KB, P1_VB, P1_END = 0, 512, 640, 768, 1280, 1792, 2304
P2_ZA, P2_ZB, P2_GA, P2_GB, P2_END = 0, 512, 1024, 2048, 3072

NA_QROWS = 4
NA_QBLK = NA_QROWS * GRID_W
NA_BAND_ROWS = 12
NA_BAND = NA_BAND_ROWS * GRID_W
NA_NBLK = GRID_ROWS // NA_QROWS
NA_UNROLL = 16

TOKEN_TILE = 512
GQA_TK = 512
GQA_UNROLL = 4
DENOM_FLOOR = 2.0 ** -64
DENOM_CEIL = 2.0 ** 100
VMEM_LIMIT = 56 * 1024 * 1024

bf16 = jnp.bfloat16
f32 = jnp.float32


def _nn(a, b):
    return lax.dot_general(a, b, (((1,), (0,)), ((), ())), preferred_element_type=f32)


def _nt(a, b):
    return lax.dot_general(a, b, (((1,), (1,)), ((), ())), preferred_element_type=f32)


def _tn(a, b):
    return lax.dot_general(a, b, (((0,), (0,)), ((), ())), preferred_element_type=f32)


def _rms_rows(x, g):
    return x * lax.rsqrt(jnp.mean(x * x, axis=-1, keepdims=True) + EPS) * g


def _sigmoid(x):
    return 1.0 / (1.0 + jnp.exp(-x))


def _params(sem):
    return pltpu.CompilerParams(dimension_semantics=sem, vmem_limit_bytes=VMEM_LIMIT)


def _proj_kernel(x_ref, g_ref, w_ref, qg_ref, kg_ref, cos_ref, sin_ref,
                 qta_ref, ka_ref, vta_ref, qtb_ref, kb_ref, vtb_ref):
    h = _rms_rows(x_ref[0], g_ref[...]).astype(bf16)
    cos = cos_ref[...]
    sin = sin_ref[...]

    def norm_rope(t, gain):
        y = t * lax.rsqrt(jnp.mean(t * t, axis=0, keepdims=True) + EPS) * gain
        sw = jnp.concatenate([y[16:32], y[0:16], y[48:64], y[32:48]], axis=0)
        return y * cos + sw * sin

    qg = qg_ref[...]
    kg = kg_ref[...]
    pt = _nt(w_ref[...], h)
    qa, ka, va = pt[P1_QA:P1_KA], pt[P1_KA:P1_VA], pt[P1_VA:P1_QB]
    qb, kbt, vb = pt[P1_QB:P1_KB], pt[P1_KB:P1_VB], pt[P1_VB:P1_END]
    for hd in range(ATTN_HEADS):
        lo = hd * HEAD_DIM
        qta_ref[0, 0, lo:lo + HEAD_DIM, :] = (norm_rope(qa[lo:lo + HEAD_DIM], qg) * Q_SCALE).astype(bf16)
    kt = jnp.concatenate([norm_rope(ka[0:HEAD_DIM], kg), norm_rope(ka[HEAD_DIM:], kg)], axis=0)
    ka_ref[0] = kt.T.astype(bf16)
    vta_ref[0] = va.astype(bf16)
    qtb_ref[0] = (qb * Q_SCALE).astype(bf16)
    for p in range(NA_HEADS // 2):
        kb_ref[0, p] = kbt[128 * p:128 * (p + 1)].T.astype(bf16)
    vtb_ref[0] = vb.astype(bf16)


def _proj_call(x3, norm_g, w1t, qg, kg, cos_t, sin_t, tm):
    bx, lx, _ = x3.shape
    grid = (bx, lx // tm)
    out_shape = (
        jax.ShapeDtypeStruct((bx, lx // tm, ATTN_WIDTH, tm), bf16),
        jax.ShapeDtypeStruct((bx, lx, KV_WIDTH), bf16),
        jax.ShapeDtypeStruct((bx, KV_WIDTH, lx), bf16),
        jax.ShapeDtypeStruct((bx, NA_WIDTH, lx), bf16),
        jax.ShapeDtypeStruct((bx, NA_HEADS // 2, lx, 128), bf16),
        jax.ShapeDtypeStruct((bx, NA_WIDTH, lx), bf16),
    )
    return pl.pallas_call(
        _proj_kernel,
        grid=grid,
        in_specs=[
            pl.BlockSpec((1, tm, D_MODEL), lambda b, j: (b, j, 0)),
            pl.BlockSpec((1, D_MODEL), lambda b, j: (0, 0)),
            pl.BlockSpec((P1_END, D_MODEL), lambda b, j: (0, 0)),
            pl.BlockSpec((HEAD_DIM, 1), lambda b, j: (0, 0)),
            pl.BlockSpec((HEAD_DIM, 1), lambda b, j: (0, 0)),
            pl.BlockSpec((HEAD_DIM, tm), lambda b, j: (0, j)),
            pl.BlockSpec((HEAD_DIM, tm), lambda b, j: (0, j)),
        ],
        out_specs=(
            pl.BlockSpec((1, 1, ATTN_WIDTH, tm), lambda b, j: (b, j, 0, 0)),
            pl.BlockSpec((1, tm, KV_WIDTH), lambda b, j: (b, j, 0)),
            pl.BlockSpec((1, KV_WIDTH, tm), lambda b, j: (b, 0, j)),
            pl.BlockSpec((1, NA_WIDTH, tm), lambda b, j: (b, 0, j)),
            pl.BlockSpec((1, NA_HEADS // 2, tm, 128), lambda b, j: (b, 0, j, 0)),
            pl.BlockSpec((1, NA_WIDTH, tm), lambda b, j: (b, 0, j)),
        ),
        out_shape=out_shape,
        compiler_params=_params(("parallel", "parallel")),
        name="qkv_proj",
    )(x3, norm_g, w1t, qg, kg, cos_t, sin_t)


def _colmax8(x):
    return jnp.max(x.reshape(x.shape[0] // 8, 8, x.shape[1]), axis=0)


def _colsum8(x):
    return jnp.sum(x.reshape(x.shape[0] // 8, 8, x.shape[1]), axis=0)


def _denominators_trusted(l_min, l_max):
    return jnp.logical_and(jnp.min(l_min) >= DENOM_FLOOR, jnp.max(l_max) <= DENOM_CEIL)


def _gqa_kernel(q_ref, k_ref, km_ref, vt_ref, vtm_ref, o_ref, s_ref, *, tq, n_qblk, heads):
    g = pl.program_id(1)
    row = lax.broadcasted_iota(jnp.int32, (2 * HEAD_DIM, tq), 0)
    sel = (row >= HEAD_DIM) == (g == 1)
    n_items = n_qblk * heads
    n_chunks = SEQ // GQA_TK
    blocks = [(0, N_META)] + [(META_PAD + c * GQA_TK, GQA_TK) for c in range(n_chunks)]

    def item_index(t):
        return t // heads, pl.multiple_of((t % heads) * HEAD_DIM, HEAD_DIM)

    def load_qext(t):
        qi, off = item_index(t)
        q = q_ref[0, qi, pl.ds(off, HEAD_DIM), :]
        return jnp.where(sel, jnp.concatenate([q, q], axis=0), jnp.zeros((), bf16))

    def score_block(qext, lo, n):
        if lo == 0:
            return _nn(km_ref[0, 0:n, :], qext)
        return _nn(k_ref[0, lo - META_PAD:lo - META_PAD + n, :], qext)

    def pv_block(p, lo, n):
        if lo == 0:
            return _nn(vtm_ref[0, :, 0:n], p.astype(bf16))
        return _nn(vt_ref[0, :, lo - META_PAD:lo - META_PAD + n], p.astype(bf16))

    def accumulate(acc, l8, p, lo, n):
        d = pv_block(p, lo, n)
        return (d, _colsum8(p)) if acc is None else (acc + d, l8 + _colsum8(p))

    def store_item(t, acc, l8):
        l = jnp.sum(l8, axis=0, keepdims=True)
        qi, off = item_index(t)
        o_ref[0, qi, pl.ds(off, HEAD_DIM), :] = (acc / l).astype(bf16)
        return l

    def fast_item(t, carry):
        l_min, l_max = carry
        qext = load_qext(t)
        acc, l8, pending = None, None, None
        for lo, n in blocks + [(None, None)]:
            s = score_block(qext, lo, n) if lo is not None else None
            if pending is not None:
                s_prev, plo, pn = pending
                acc, l8 = accumulate(acc, l8, jnp.exp2(s_prev), plo, pn)
            pending = (s, lo, n)
        l = store_item(t, acc, l8)
        return jnp.minimum(l_min, l), jnp.maximum(l_max, l)

    ones = jnp.full((1, tq), 1.0, f32)
    l_min, l_max = lax.fori_loop(0, n_items, fast_item, (ones, ones), unroll=min(GQA_UNROLL, n_items))

    def exact_item(t, carry):
        qext = load_qext(t)
        m8 = None
        for lo, n in blocks:
            s = score_block(qext, lo, n)
            s_ref[lo:lo + n, :] = s
            m8 = _colmax8(s) if m8 is None else jnp.maximum(m8, _colmax8(s))
        m = jnp.max(m8, axis=0, keepdims=True)
        acc, l8 = None, None
        for lo, n in blocks:
            acc, l8 = accumulate(acc, l8, jnp.exp2(s_ref[lo:lo + n, :] - m), lo, n)
        store_item(t, acc, l8)
        return carry

    @pl.when(jnp.logical_not(_denominators_trusted(l_min, l_max)))
    def _():
        lax.fori_loop(0, n_items, exact_item, 0)


def _pack_meta_queries(qt_meta, b):
    q = qt_meta.reshape(-1, ATTN_WIDTH, TOKEN_TILE // META_PAD, META_PAD)
    q = q.transpose(0, 2, 1, 3).reshape(b, ATTN_KV_HEADS, ATTN_GROUP, HEAD_DIM, META_PAD)
    q = q[..., :N_META].transpose(0, 1, 3, 2, 4).reshape(b, KV_WIDTH, ATTN_GROUP * N_META)
    return jnp.pad(q, ((0, 0), (0, 0), (0, META_PAD - ATTN_GROUP * N_META)))[:, None]


def _unpack_meta_outputs(o_packed, like):
    b = o_packed.shape[0]
    o = o_packed[:, 0, :, :ATTN_GROUP * N_META].reshape(b, ATTN_KV_HEADS, HEAD_DIM, ATTN_GROUP, N_META)
    o = o.transpose(0, 1, 3, 2, 4).reshape(b, ATTN_WIDTH, N_META)
    o = jnp.pad(o, ((0, 0), (0, 0), (0, META_PAD - N_META)))
    per_tile = TOKEN_TILE // META_PAD
    o = o.reshape(b // per_tile, per_tile, ATTN_WIDTH, META_PAD).transpose(0, 2, 1, 3)
    return o.reshape(like.shape)


def _gqa_call(qt, k_real, k_meta, vt_real, vt_meta, *, meta_queries):
    b = k_real.shape[0]
    if meta_queries:
        tq, n_qblk, heads = META_PAD, 1, 1
    else:
        tq, n_qblk, heads = TOKEN_TILE, SEQ // TOKEN_TILE, ATTN_GROUP
    gw = heads * HEAD_DIM
    q_map = lambda bi, g: (bi, 0, g, 0)
    n_keys = META_PAD + SEQ
    return pl.pallas_call(
        functools.partial(_gqa_kernel, tq=tq, n_qblk=n_qblk, heads=heads),
        grid=(b, ATTN_KV_HEADS),
        in_specs=[
            pl.BlockSpec((1, n_qblk, gw, tq), q_map),
            pl.BlockSpec((1, SEQ, KV_WIDTH), lambda bi, g: (bi, 0, 0)),
            pl.BlockSpec((1, META_PAD, KV_WIDTH), lambda bi, g: (0, bi, 0)),
            pl.BlockSpec((1, HEAD_DIM, SEQ), lambda bi, g: (bi, g, 0)),
            pl.BlockSpec((1, HEAD_DIM, META_PAD), lambda bi, g: (0, g, bi)),
        ],
        out_specs=pl.BlockSpec((1, n_qblk, gw, tq), q_map),
        out_shape=jax.ShapeDtypeStruct(qt.shape, bf16),
        scratch_shapes=[pltpu.VMEM((n_keys, tq), f32)],
        compiler_params=_params(("parallel", "parallel")),
        name="gqa_meta" if meta_queries else "gqa_attn",
    )(qt, k_real, k_meta, vt_real, vt_meta)


def _na_block_geometry(blk):
    r0 = blk * NA_QROWS
    kb = min(max(r0 - NA_KH // 2, 0), GRID_ROWS - NA_BAND_ROWS)
    typ = 0 if blk == 0 else (2 if blk == NA_NBLK - 1 else 1)
    return r0, kb, typ


def _na_row_window(r):
    rs = min(max(r - NA_KH // 2, 0), GRID_ROWS - NA_KH)
    return rs, rs + NA_KH


def _na_build_tables(rpb_ref, h, t_ref, tab_ref):
    kc = lax.broadcasted_iota(jnp.int32, (GRID_W, 128), 0)
    lane = lax.broadcasted_iota(jnp.int32, (GRID_W, 128), 1)
    c = lane % GRID_W
    rel = kc - c + (NA_KW - 1)
    cs = jnp.clip(c - NA_KW // 2, 0, GRID_W - NA_KW)
    cvalid = (kc >= cs) & (kc < cs + NA_KW)
    neg_tile = jnp.full((GRID_W, 128), NEG, f32)

    def build_a(a, carry):
        acc = neg_tile
        for r in range(2 * NA_KW - 1):
            acc = jnp.where(rel == r, rpb_ref[h, a, r] * LOG2E, acc)
        t_ref[a] = jnp.where(cvalid, acc, NEG)
        return carry

    lax.fori_loop(0, 2 * NA_KH - 1, build_a, 0)
    for typ, blk in enumerate((0, 1, NA_NBLK - 1)):
        r0, kb, _ = _na_block_geometry(blk)
        for i in range(NA_BAND_ROWS):
            kr = kb + i
            for jj in range(NA_QROWS // 2):
                halves = []
                for r in (r0 + 2 * jj, r0 + 2 * jj + 1):
                    lo, hi = _na_row_window(r)
                    halves.append(t_ref[kr - r + NA_KH - 1] if lo <= kr < hi else neg_tile)
                tab_ref[typ, i * GRID_W:(i + 1) * GRID_W, jj * 128:(jj + 1) * 128] = jnp.where(
                    lane < GRID_W, halves[0], halves[1])


def _na_kernel(rpb_ref, q_ref, k_ref, km_ref, vt_ref, vtm_ref, o_ref,
               t_ref, tab_ref, s_ref):
    @pl.when(pl.program_id(1) == 0)
    def _():
        _na_build_tables(rpb_ref, pl.program_id(0), t_ref, tab_ref)

    par = pl.program_id(0) % 2
    row = lax.broadcasted_iota(jnp.int32, (2 * HEAD_DIM, NA_QBLK), 0)
    sel = (row >= HEAD_DIM) == (par == 1)

    def band_start(blk):
        kb = jnp.clip(blk * NA_QROWS - NA_KH // 2, 0, GRID_ROWS - NA_BAND_ROWS)
        return pl.multiple_of(kb * GRID_W, 2 * GRID_W)

    def table_type(blk):
        return jnp.where(blk == 0, 0, jnp.where(blk == NA_NBLK - 1, 2, 1))

    def q_lanes(blk):
        return pl.ds(pl.multiple_of(blk * NA_QBLK, NA_QBLK), NA_QBLK)

    def load_qext(blk):
        q = q_ref[0, :, q_lanes(blk)]
        return jnp.where(sel, jnp.concatenate([q, q], axis=0), jnp.zeros((), bf16))

    def block_scores(blk, qext):
        sm = _nn(km_ref[0, 0, 0:N_META, :], qext)
        s = _nn(k_ref[0, 0, pl.ds(band_start(blk), NA_BAND), :], qext) + tab_ref[table_type(blk)]
        return sm, s

    def block_output(blk, pm, p):
        acc = (_nn(vtm_ref[0, :, 0:N_META], pm.astype(bf16))
               + _nn(vt_ref[0, :, pl.ds(band_start(blk), NA_BAND)], p.astype(bf16)))
        l = jnp.sum(_colsum8(pm) + _colsum8(p), axis=0, keepdims=True)
        o_ref[0, :, q_lanes(blk)] = (acc / l).astype(bf16)
        return l

    def fast_stage(u, carry):
        l_min, l_max = carry
        pending = None
        for j in range(NA_UNROLL + 1):
            if j < NA_UNROLL:
                blk = u * NA_UNROLL + j
                nxt = (blk,) + block_scores(blk, load_qext(blk))
            if pending is not None:
                pblk, sm, s = pending
                l = block_output(pblk, jnp.exp2(sm), jnp.exp2(s))
                l_min, l_max = jnp.minimum(l_min, l), jnp.maximum(l_max, l)
            pending = nxt
        return l_min, l_max

    ones = jnp.full((1, NA_QBLK), 1.0, f32)
    l_min, l_max = lax.fori_loop(0, NA_NBLK // NA_UNROLL, fast_stage, (ones, ones))

    def exact_block(blk, carry):
        sm, s = block_scores(blk, load_qext(blk))
        s_ref[0:N_META, :] = sm
        s_ref[META_PAD:, :] = s
        m = jnp.max(jnp.maximum(_colmax8(sm), _colmax8(s)), axis=0, keepdims=True)
        block_output(blk, jnp.exp2(s_ref[0:N_META, :] - m), jnp.exp2(s_ref[META_PAD:, :] - m))
        return carry

    @pl.when(jnp.logical_not(_denominators_trusted(l_min, l_max)))
    def _():
        lax.fori_loop(0, NA_NBLK, exact_block, 0)


def _na_call(qt, k_real, k_meta, vt_real, vt_meta, rpb):
    b = qt.shape[0]
    return pl.pallas_call(
        _na_kernel,
        grid=(NA_HEADS, b),
        in_specs=[
            pl.BlockSpec(memory_space=pltpu.SMEM),
            pl.BlockSpec((1, HEAD_DIM, SEQ), lambda h, bi: (bi, h, 0)),
            pl.BlockSpec((1, 1, SEQ, 128), lambda h, bi: (bi, h // 2, 0, 0)),
            pl.BlockSpec((1, 1, META_PAD, 128), lambda h, bi: (0, h // 2, bi, 0)),
            pl.BlockSpec((1, HEAD_DIM, SEQ), lambda h, bi: (bi, h, 0)),
            pl.BlockSpec((1, HEAD_DIM, META_PAD), lambda h, bi: (0, h, bi)),
        ],
        out_specs=pl.BlockSpec((1, HEAD_DIM, SEQ), lambda h, bi: (bi, h, 0)),
        out_shape=jax.ShapeDtypeStruct(qt.shape, bf16),
        scratch_shapes=[
            pltpu.VMEM((2 * NA_KH - 1, GRID_W, 128), f32),
            pltpu.VMEM((3, NA_BAND, NA_QBLK), f32),
            pltpu.VMEM((META_PAD + NA_BAND, NA_QBLK), f32),
        ],
        compiler_params=_params(("arbitrary", "arbitrary")),
        name="na_attn",
    )(rpb, qt, k_real, k_meta, vt_real, vt_meta)


def _na_meta_kernel(q_ref, km_ref, vtm_ref, o_ref):
    row = lax.broadcasted_iota(jnp.int32, (2 * HEAD_DIM, META_PAD), 0)
    meta_valid = row < N_META
    for h in range(NA_HEADS):
        lo = h * HEAD_DIM
        q = q_ref[0, lo:lo + HEAD_DIM, :]
        sel = (row >= HEAD_DIM) == (h % 2 == 1)
        qext = jnp.where(sel, jnp.concatenate([q, q], axis=0), jnp.zeros((), bf16))
        s = jnp.where(meta_valid, _nn(km_ref[0, h // 2], qext), NEG)
        m = jnp.max(s, axis=0, keepdims=True)
        p = jnp.exp2(s - m)
        l = jnp.sum(p, axis=0, keepdims=True)
        acc = _nn(vtm_ref[0, lo:lo + HEAD_DIM, :], p.astype(bf16))
        o_ref[0, lo:lo + HEAD_DIM, :] = (acc / l).astype(bf16)


def _na_meta_call(qt_meta, k_meta, vt_meta, b):
    return pl.pallas_call(
        _na_meta_kernel,
        grid=(b,),
        in_specs=[
            pl.BlockSpec((1, NA_WIDTH, META_PAD), lambda bi: (0, 0, bi)),
            pl.BlockSpec((1, NA_HEADS // 2, META_PAD, 128), lambda bi: (0, 0, bi, 0)),
            pl.BlockSpec((1, NA_WIDTH, META_PAD), lambda bi: (0, 0, bi)),
        ],
        out_specs=pl.BlockSpec((1, NA_WIDTH, META_PAD), lambda bi: (0, 0, bi)),
        out_shape=jax.ShapeDtypeStruct(qt_meta.shape, bf16),
        compiler_params=_params(("parallel",)),
        name="na_meta",
    )(qt_meta, k_meta, vt_meta)


def _out_kernel(x_ref, g_ref, ata_ref, atb_ref, w2_ref, woa_ref, wob_ref, wout_ref, fg_ref,
                o_ref, *, final):
    x = x_ref[0]
    h = _rms_rows(x, g_ref[...]).astype(bf16)

    def gated(at_ref, z):
        return (at_ref[...].reshape(z.shape).astype(f32) * (z * _sigmoid(z))).astype(bf16)

    zg = _nt(w2_ref[...], h)
    za, zb, ga, gb = zg[P2_ZA:P2_ZB], zg[P2_ZB:P2_GA], zg[P2_GA:P2_GB], zg[P2_GB:P2_END]
    ya = _nn(woa_ref[...], gated(ata_ref, za))
    yb = _nn(wob_ref[...], gated(atb_ref, zb))
    mix = (_sigmoid(ga) * ya + _sigmoid(gb) * yb).astype(bf16)
    out = x + _tn(mix, wout_ref[...])
    if final:
        out = _rms_rows(out, fg_ref[...])
    o_ref[0] = out


def _out_call(x3, norm_g, at_a, at_b, w2t, woat, wobt, wout, final_g, tm, final):
    bx, lx, _ = x3.shape
    const = lambda b, j: (0, 0)
    return pl.pallas_call(
        functools.partial(_out_kernel, final=final),
        grid=(bx, lx // tm),
        in_specs=[
            pl.BlockSpec((1, tm, D_MODEL), lambda b, j: (b, j, 0)),
            pl.BlockSpec((1, D_MODEL), const),
            pl.BlockSpec((1, 1, ATTN_WIDTH, tm), lambda b, j: (b, j, 0, 0)),
            pl.BlockSpec((1, NA_WIDTH, tm), lambda b, j: (b, 0, j)),
            pl.BlockSpec((P2_END, D_MODEL), const),
            pl.BlockSpec((D_MODEL, ATTN_WIDTH), const),
            pl.BlockSpec((D_MODEL, NA_WIDTH), const),
            pl.BlockSpec((D_MODEL, D_MODEL), const),
            pl.BlockSpec((1, D_MODEL), const),
        ],
        out_specs=pl.BlockSpec((1, tm, D_MODEL), lambda b, j: (b, j, 0)),
        out_shape=jax.ShapeDtypeStruct(x3.shape, f32),
        compiler_params=_params(("parallel", "parallel")),
        name="gate_out_proj",
    )(x3, norm_g, at_a, at_b, w2t, woat, wobt, wout, final_g)


def _rope_tables():
    t = np.arange(SEQ)
    axis_dim = HEAD_DIM // 2
    inv = jnp.asarray(ROPE_THETA, f32) ** (-jnp.arange(0, axis_dim, 2, dtype=f32) / axis_dim)
    ang_r = jnp.asarray(t // GRID_W, f32)[None, :] * inv[:, None]
    ang_c = jnp.asarray(t % GRID_W, f32)[None, :] * inv[:, None]
    ang = jnp.concatenate([ang_r, ang_r, ang_c, ang_c], axis=0)
    sign = np.repeat(np.array([-1.0, 1.0, -1.0, 1.0], np.float32), axis_dim // 2)[:, None]
    return jnp.cos(ang), jnp.sin(ang) * sign


def kernel(x, meta_tokens, norm_g, w_in, q_norm_g, k_norm_g, na_rpb, w_o_attn, w_o_na, w_out,
           final_norm_g):
    b = x.shape[0]
    depth = w_in.shape[0]
    cos_r, sin_r = _rope_tables()
    cos_m = jnp.ones((HEAD_DIM, b * META_PAD), f32)
    sin_m = jnp.zeros((HEAD_DIM, b * META_PAD), f32)

    meta_pad = jnp.concatenate(
        [meta_tokens.astype(f32), jnp.zeros((META_PAD - N_META, D_MODEL), f32)], axis=0)
    xm = jnp.tile(meta_pad, (b, 1))[None]
    xr = x
    fg = final_norm_g.reshape(1, D_MODEL)

    for i in range(depth):
        wi = w_in[i]
        qa, ka, va, za, qb, kb, vb, zb, ga, gb = (
            wi[:, 0:512], wi[:, 512:640], wi[:, 640:768], wi[:, 768:1280], wi[:, 1280:1792],
            wi[:, 1792:2304], wi[:, 2304:2816], wi[:, 2816:3328], wi[:, 3328:4352], wi[:, 4352:5376])
        w1t = jnp.concatenate([qa, ka, va, qb, kb, vb], axis=1).T.astype(bf16)
        w2t = jnp.concatenate([za, zb, ga, gb], axis=1).T.astype(bf16)
        woat = w_o_attn[i].T.astype(bf16)
        wobt = w_o_na[i].T.astype(bf16)
        wout = w_out[i].astype(bf16)
        ng = norm_g[i].reshape(1, D_MODEL)
        qg = q_norm_g[i].reshape(HEAD_DIM, 1)
        kg = k_norm_g[i].reshape(HEAD_DIM, 1)
        last = i == depth - 1

        qta, k_a, vta, qtb, k_b, vtb = _proj_call(xr, ng, w1t, qg, kg, cos_r, sin_r, TOKEN_TILE)
        qta_m, k_a_m, vta_m, qtb_m, k_b_m, vtb_m = _proj_call(
            xm, ng, w1t, qg, kg, cos_m, sin_m, TOKEN_TILE)

        at_a = _gqa_call(qta, k_a, k_a_m, vta, vta_m, meta_queries=False)
        at_b = _na_call(qtb, k_b, k_b_m, vtb, vtb_m, na_rpb[i])
        xr_new = _out_call(xr, ng, at_a, at_b, w2t, woat, wobt, wout, fg, TOKEN_TILE, last)
        if not last:
            at_a_m = _unpack_meta_outputs(
                _gqa_call(_pack_meta_queries(qta_m, b), k_a, k_a_m, vta, vta_m, meta_queries=True),
                qta_m)
            at_b_m = _na_meta_call(qtb_m, k_b_m, vtb_m, b)
            xm = _out_call(xm, ng, at_a_m, at_b_m, w2t, woat, wobt, wout, fg, TOKEN_TILE, False)
        xr = xr_new
    return xr
```

```python
import functools

import numpy as np
import jax
import jax.numpy as jnp
from jax import lax
from jax.experimental import pallas as pl
from jax.experimental.pallas import tpu as pltpu

D_MODEL = 1024
SEQ = 4096
GRID_W = 64
GRID_ROWS = SEQ // GRID_W
N_META = 16
META_PAD = 128
HEAD_DIM = 64
ATTN_HEADS = 8
ATTN_KV_HEADS = 2
ATTN_GROUP = ATTN_HEADS // ATTN_KV_HEADS
NA_HEADS = 8
NA_KH = 8
NA_KW = 16
ROPE_THETA = 10000.0
EPS = 1e-6
SCALE = HEAD_DIM ** -0.5
LOG2E = 1.4426950408889634
Q_SCALE = SCALE * LOG2E
NEG = -1e30

ATTN_WIDTH = ATTN_HEADS * HEAD_DIM
KV_WIDTH = ATTN_KV_HEADS * HEAD_DIM
NA_WIDTH = NA_HEADS * HEAD_DIM

P1_QA, P1_KA, P1_VA, P1_QB, P1_KB, P1_VB, P1_END = 0, 512, 640, 768, 1280, 1792, 2304
P2_ZA, P2_ZB, P2_GA, P2_GB, P2_END = 0, 512, 1024, 2048, 3072

NA_QROWS = 4
NA_QBLK = NA_QROWS * GRID_W
NA_BAND_ROWS = 12
NA_BAND = NA_BAND_ROWS * GRID_W
NA_NBLK = GRID_ROWS // NA_QROWS
NA_UNROLL = 16

TOKEN_TILE = 512
GQA_TK = 512
GQA_UNROLL = 4
DENOM_FLOOR = 2.0 ** -64
DENOM_CEIL = 2.0 ** 100
VMEM_LIMIT = 56 * 1024 * 1024

bf16 = jnp.bfloat16
f32 = jnp.float32


def _nn(a, b):
    return lax.dot_general(a, b, (((1,), (0,)), ((), ())), preferred_element_type=f32)


def _nt(a, b):
    return lax.dot_general(a, b, (((1,), (1,)), ((), ())), preferred_element_type=f32)


def _tn(a, b):
    return lax.dot_general(a, b, (((0,), (0,)), ((), ())), preferred_element_type=f32)


def _rms_rows(x, g):
    return x * lax.rsqrt(jnp.mean(x * x, axis=-1, keepdims=True) + EPS) * g


def _sigmoid(x):
    return 1.0 / (1.0 + jnp.exp(-x))


def _params(sem):
    return pltpu.CompilerParams(dimension_semantics=sem, vmem_limit_bytes=VMEM_LIMIT)


def _proj_kernel(x_ref, g_ref, w_ref, qg_ref, kg_ref, cos_ref, sin_ref,
                 qta_ref, ka_ref, vta_ref, qtb_ref, kb_ref, vtb_ref, ht_ref):
    ht_ref[...] = _rms_rows(x_ref[0], g_ref[...]).T.astype(bf16)
    cos = cos_ref[...]
    sin = sin_ref[...]

    def norm_rope(t, gain):
        y = t * lax.rsqrt(jnp.mean(t * t, axis=0, keepdims=True) + EPS) * gain
        sw = jnp.concatenate([y[16:32], y[0:16], y[48:64], y[32:48]], axis=0)
        return y * cos + sw * sin

    qg = qg_ref[...]
    kg = kg_ref[...]
    def proj(lo, hi):
        return _nn(w_ref[lo:hi, :], ht_ref[...])

    pa = proj(P1_QA, P1_QB)
    qa, ka, va = pa[P1_QA:P1_KA], pa[P1_KA:P1_VA], pa[P1_VA:P1_QB]
    qb = proj(P1_QB, P1_KB)
    kbt = proj(P1_KB, P1_VB)
    vb = proj(P1_VB, P1_END)
    for hd in range(ATTN_HEADS):
        lo = hd * HEAD_DIM
        qta_ref[0, 0, lo:lo + HEAD_DIM, :] = (norm_rope(qa[lo:lo + HEAD_DIM], qg) * Q_SCALE).astype(bf16)
    kt = jnp.concatenate([norm_rope(ka[0:HEAD_DIM], kg), norm_rope(ka[HEAD_DIM:], kg)], axis=0)
    ka_ref[0] = kt.T.astype(bf16)
    vta_ref[0] = va.astype(bf16)
    qtb_ref[0] = (qb * Q_SCALE).astype(bf16)
    for p in range(NA_HEADS // 2):
        kb_ref[0, p] = kbt[128 * p:128 * (p + 1)].T.astype(bf16)
    vtb_ref[0] = vb.astype(bf16)


def _proj_call(x3, norm_g, w1t, qg, kg, cos_t, sin_t, tm):
    bx, lx, _ = x3.shape
    grid = (bx, lx // tm)
    out_shape = (
        jax.ShapeDtypeStruct((bx, lx // tm, ATTN_WIDTH, tm), bf16),
        jax.ShapeDtypeStruct((bx, lx, KV_WIDTH), bf16),
        jax.ShapeDtypeStruct((bx, KV_WIDTH, lx), bf16),
        jax.ShapeDtypeStruct((bx, NA_WIDTH, lx), bf16),
        jax.ShapeDtypeStruct((bx, NA_HEADS // 2, lx, 128), bf16),
        jax.ShapeDtypeStruct((bx, NA_WIDTH, lx), bf16),
    )
    return pl.pallas_call(
        _proj_kernel,
        grid=grid,
        in_specs=[
            pl.BlockSpec((1, tm, D_MODEL), lambda b, j: (b, j, 0)),
            pl.BlockSpec((1, D_MODEL), lambda b, j: (0, 0)),
            pl.BlockSpec((P1_END, D_MODEL), lambda b, j: (0, 0)),
            pl.BlockSpec((HEAD_DIM, 1), lambda b, j: (0, 0)),
            pl.BlockSpec((HEAD_DIM, 1), lambda b, j: (0, 0)),
            pl.BlockSpec((HEAD_DIM, tm), lambda b, j: (0, j)),
            pl.BlockSpec((HEAD_DIM, tm), lambda b, j: (0, j)),
        ],
        out_specs=(
            pl.BlockSpec((1, 1, ATTN_WIDTH, tm), lambda b, j: (b, j, 0, 0)),
            pl.BlockSpec((1, tm, KV_WIDTH), lambda b, j: (b, j, 0)),
            pl.BlockSpec((1, KV_WIDTH, tm), lambda b, j: (b, 0, j)),
            pl.BlockSpec((1, NA_WIDTH, tm), lambda b, j: (b, 0, j)),
            pl.BlockSpec((1, NA_HEADS // 2, tm, 128), lambda b, j: (b, 0, j, 0)),
            pl.BlockSpec((1, NA_WIDTH, tm), lambda b, j: (b, 0, j)),
        ),
        out_shape=out_shape,
        scratch_shapes=[pltpu.VMEM((D_MODEL, tm), bf16)],
        compiler_params=_params(("parallel", "parallel")),
        name="qkv_proj",
    )(x3, norm_g, w1t, qg, kg, cos_t, sin_t)


def _colmax8(x):
    return jnp.max(x.reshape(x.shape[0] // 8, 8, x.shape[1]), axis=0)


def _colsum8(x):
    return jnp.sum(x.reshape(x.shape[0] // 8, 8, x.shape[1]), axis=0)


def _denominators_trusted(l_min, l_max):
    return jnp.logical_and(jnp.min(l_min) >= DENOM_FLOOR, jnp.max(l_max) <= DENOM_CEIL)


def _gqa_kernel(q_ref, k_ref, km_ref, vt_ref, vtm_ref, o_ref, s_ref, *, tq, n_qblk, heads):
    g = pl.program_id(1)
    row = lax.broadcasted_iota(jnp.int32, (2 * HEAD_DIM, tq), 0)
    sel = (row >= HEAD_DIM) == (g == 1)
    n_items = n_qblk * heads
    n_chunks = SEQ // GQA_TK
    blocks = [(0, N_META)] + [(META_PAD + c * GQA_TK, GQA_TK) for c in range(n_chunks)]

    def item_index(t):
        return t // heads, pl.multiple_of((t % heads) * HEAD_DIM, HEAD_DIM)

    def load_qext(t):
        qi, off = item_index(t)
        q = q_ref[0, qi, pl.ds(off, HEAD_DIM), :]
        return jnp.where(sel, jnp.concatenate([q, q], axis=0), jnp.zeros((), bf16))

    def score_block(qext, lo, n):
        if lo == 0:
            return _nn(km_ref[0, 0:n, :], qext)
        return _nn(k_ref[0, lo - META_PAD:lo - META_PAD + n, :], qext)

    def pv_block(p, lo, n):
        if lo == 0:
            return _nn(vtm_ref[0, :, 0:n], p.astype(bf16))
        return _nn(vt_ref[0, :, lo - META_PAD:lo - META_PAD + n], p.astype(bf16))

    def accumulate(acc, l8, p, lo, n):
        d = pv_block(p, lo, n)
        return (d, _colsum8(p)) if acc is None else (acc + d, l8 + _colsum8(p))

    def store_item(t, acc, l8):
        l = jnp.sum(l8, axis=0, keepdims=True)
        qi, off = item_index(t)
        o_ref[0, qi, pl.ds(off, HEAD_DIM), :] = (acc / l).astype(bf16)
        return l

    def fast_item(t, carry):
        l_min, l_max = carry
        qext = load_qext(t)
        acc, l8, pending = None, None, None
        for lo, n in blocks + [(None, None)]:
            s = score_block(qext, lo, n) if lo is not None else None
            if pending is not None:
                s_prev, plo, pn = pending
                acc, l8 = accumulate(acc, l8, jnp.exp2(s_prev), plo, pn)
            pending = (s, lo, n)
        l = store_item(t, acc, l8)
        return jnp.minimum(l_min, l), jnp.maximum(l_max, l)

    ones = jnp.full((1, tq), 1.0, f32)
    l_min, l_max = lax.fori_loop(0, n_items, fast_item, (ones, ones), unroll=min(GQA_UNROLL, n_items))

    def exact_item(t, carry):
        qext = load_qext(t)
        m8 = None
        for lo, n in blocks:
            s = score_block(qext, lo, n)
            s_ref[lo:lo + n, :] = s
            m8 = _colmax8(s) if m8 is None else jnp.maximum(m8, _colmax8(s))
        m = jnp.max(m8, axis=0, keepdims=True)
        acc, l8 = None, None
        for lo, n in blocks:
            acc, l8 = accumulate(acc, l8, jnp.exp2(s_ref[lo:lo + n, :] - m), lo, n)
        store_item(t, acc, l8)
        return carry

    @pl.when(jnp.logical_not(_denominators_trusted(l_min, l_max)))
    def _():
        lax.fori_loop(0, n_items, exact_item, 0)


def _pack_meta_queries(qt_meta, b):
    q = qt_meta.reshape(-1, ATTN_WIDTH, TOKEN_TILE // META_PAD, META_PAD)
    q = q.transpose(0, 2, 1, 3).reshape(b, ATTN_KV_HEADS, ATTN_GROUP, HEAD_DIM, META_PAD)
    q = q[..., :N_META].transpose(0, 1, 3, 2, 4).reshape(b, KV_WIDTH, ATTN_GROUP * N_META)
    return jnp.pad(q, ((0, 0), (0, 0), (0, META_PAD - ATTN_GROUP * N_META)))[:, None]


def _unpack_meta_outputs(o_packed, like):
    b = o_packed.shape[0]
    o = o_packed[:, 0, :, :ATTN_GROUP * N_META].reshape(b, ATTN_KV_HEADS, HEAD_DIM, ATTN_GROUP, N_META)
    o = o.transpose(0, 1, 3, 2, 4).reshape(b, ATTN_WIDTH, N_META)
    o = jnp.pad(o, ((0, 0), (0, 0), (0, META_PAD - N_META)))
    per_tile = TOKEN_TILE // META_PAD
    o = o.reshape(b // per_tile, per_tile, ATTN_WIDTH, META_PAD).transpose(0, 2, 1, 3)
    return o.reshape(like.shape)


def _gqa_call(qt, k_real, k_meta, vt_real, vt_meta, *, meta_queries):
    b = k_real.shape[0]
    if meta_queries:
        tq, n_qblk, heads = META_PAD, 1, 1
    else:
        tq, n_qblk, heads = TOKEN_TILE, SEQ // TOKEN_TILE, ATTN_GROUP
    gw = heads * HEAD_DIM
    q_map = lambda bi, g: (bi, 0, g, 0)
    n_keys = META_PAD + SEQ
    return pl.pallas_call(
        functools.partial(_gqa_kernel, tq=tq, n_qblk=n_qblk, heads=heads),
        grid=(b, ATTN_KV_HEADS),
        in_specs=[
            pl.BlockSpec((1, n_qblk, gw, tq), q_map),
            pl.BlockSpec((1, SEQ, KV_WIDTH), lambda bi, g: (bi, 0, 0)),
            pl.BlockSpec((1, META_PAD, KV_WIDTH), lambda bi, g: (0, bi, 0)),
            pl.BlockSpec((1, HEAD_DIM, SEQ), lambda bi, g: (bi, g, 0)),
            pl.BlockSpec((1, HEAD_DIM, META_PAD), lambda bi, g: (0, g, bi)),
        ],
        out_specs=pl.BlockSpec((1, n_qblk, gw, tq), q_map),
        out_shape=jax.ShapeDtypeStruct(qt.shape, bf16),
        scratch_shapes=[pltpu.VMEM((n_keys, tq), f32)],
        compiler_params=_params(("parallel", "parallel")),
        name="gqa_meta" if meta_queries else "gqa_attn",
    )(qt, k_real, k_meta, vt_real, vt_meta)


def _na_block_geometry(blk):
    r0 = blk * NA_QROWS
    kb = min(max(r0 - NA_KH // 2, 0), GRID_ROWS - NA_BAND_ROWS)
    typ = 0 if blk == 0 else (2 if blk == NA_NBLK - 1 else 1)
    return r0, kb, typ


def _na_row_window(r):
    rs = min(max(r - NA_KH // 2, 0), GRID_ROWS - NA_KH)
    return rs, rs + NA_KH


def _na_build_tables(rpb_ref, h, t_ref, tab_ref):
    kc = lax.broadcasted_iota(jnp.int32, (GRID_W, 128), 0)
    lane = lax.broadcasted_iota(jnp.int32, (GRID_W, 128), 1)
    c = lane % GRID_W
    rel = kc - c + (NA_KW - 1)
    cs = jnp.clip(c - NA_KW // 2, 0, GRID_W - NA_KW)
    cvalid = (kc >= cs) & (kc < cs + NA_KW)
    neg_tile = jnp.full((GRID_W, 128), NEG, f32)

    def build_a(a, carry):
        acc = neg_tile
        for r in range(2 * NA_KW - 1):
            acc = jnp.where(rel == r, rpb_ref[h, a, r] * LOG2E, acc)
        t_ref[a] = jnp.where(cvalid, acc, NEG)
        return carry

    lax.fori_loop(0, 2 * NA_KH - 1, build_a, 0)
    for typ, blk in enumerate((0, 1, NA_NBLK - 1)):
        r0, kb, _ = _na_block_geometry(blk)
        for i in range(NA_BAND_ROWS):
            kr = kb + i
            for jj in range(NA_QROWS // 2):
                halves = []
                for r in (r0 + 2 * jj, r0 + 2 * jj + 1):
                    lo, hi = _na_row_window(r)
                    halves.append(t_ref[kr - r + NA_KH - 1] if lo <= kr < hi else neg_tile)
                tab_ref[typ, i * GRID_W:(i + 1) * GRID_W, jj * 128:(jj + 1) * 128] = jnp.where(
                    lane < GRID_W, halves[0], halves[1])


def _na_kernel(rpb_ref, q_ref, k_ref, km_ref, vt_ref, vtm_ref, o_ref,
               t_ref, tab_ref, s_ref):
    @pl.when(pl.program_id(1) == 0)
    def _():
        _na_build_tables(rpb_ref, pl.program_id(0), t_ref, tab_ref)

    par = pl.program_id(0) % 2
    row = lax.broadcasted_iota(jnp.int32, (2 * HEAD_DIM, NA_QBLK), 0)
    sel = (row >= HEAD_DIM) == (par == 1)

    def band_start(blk):
        kb = jnp.clip(blk * NA_QROWS - NA_KH // 2, 0, GRID_ROWS - NA_BAND_ROWS)
        return pl.multiple_of(kb * GRID_W, 2 * GRID_W)

    def table_type(blk):
        return jnp.where(blk == 0, 0, jnp.where(blk == NA_NBLK - 1, 2, 1))

    def q_lanes(blk):
        return pl.ds(pl.multiple_of(blk * NA_QBLK, NA_QBLK), NA_QBLK)

    def load_qext(blk):
        q = q_ref[0, :, q_lanes(blk)]
        return jnp.where(sel, jnp.concatenate([q, q], axis=0), jnp.zeros((), bf16))

    def block_scores(blk, qext):
        sm = _nn(km_ref[0, 0, 0:N_META, :], qext)
        s = _nn(k_ref[0, 0, pl.ds(band_start(blk), NA_BAND), :], qext) + tab_ref[table_type(blk)]
        return sm, s

    def block_output(blk, pm, p):
        acc = (_nn(vtm_ref[0, :, 0:N_META], pm.astype(bf16))
               + _nn(vt_ref[0, :, pl.ds(band_start(blk), NA_BAND)], p.astype(bf16)))
        l = jnp.sum(_colsum8(pm) + _colsum8(p), axis=0, keepdims=True)
        o_ref[0, :, q_lanes(blk)] = (acc / l).astype(bf16)
        return l

    def fast_stage(u, carry):
        l_min, l_max = carry
        pending = None
        for j in range(NA_UNROLL + 1):
            if j < NA_UNROLL:
                blk = u * NA_UNROLL + j
                nxt = (blk,) + block_scores(blk, load_qext(blk))
            if pending is not None:
                pblk, sm, s = pending
                l = block_output(pblk, jnp.exp2(sm), jnp.exp2(s))
                l_min, l_max = jnp.minimum(l_min, l), jnp.maximum(l_max, l)
            pending = nxt
        return l_min, l_max

    ones = jnp.full((1, NA_QBLK), 1.0, f32)
    l_min, l_max = lax.fori_loop(0, NA_NBLK // NA_UNROLL, fast_stage, (ones, ones))

    def exact_block(blk, carry):
        sm, s = block_scores(blk, load_qext(blk))
        s_ref[0:N_META, :] = sm
        s_ref[META_PAD:, :] = s
        m = jnp.max(jnp.maximum(_colmax8(sm), _colmax8(s)), axis=0, keepdims=True)
        block_output(blk, jnp.exp2(s_ref[0:N_META, :] - m), jnp.exp2(s_ref[META_PAD:, :] - m))
        return carry

    @pl.when(jnp.logical_not(_denominators_trusted(l_min, l_max)))
    def _():
        lax.fori_loop(0, NA_NBLK, exact_block, 0)


def _na_call(qt, k_real, k_meta, vt_real, vt_meta, rpb):
    b = qt.shape[0]
    return pl.pallas_call(
        _na_kernel,
        grid=(NA_HEADS, b),
        in_specs=[
            pl.BlockSpec(memory_space=pltpu.SMEM),
            pl.BlockSpec((1, HEAD_DIM, SEQ), lambda h, bi: (bi, h, 0)),
            pl.BlockSpec((1, 1, SEQ, 128), lambda h, bi: (bi, h // 2, 0, 0)),
            pl.BlockSpec((1, 1, META_PAD, 128), lambda h, bi: (0, h // 2, bi, 0)),
            pl.BlockSpec((1, HEAD_DIM, SEQ), lambda h, bi: (bi, h, 0)),
            pl.BlockSpec((1, HEAD_DIM, META_PAD), lambda h, bi: (0, h, bi)),
        ],
        out_specs=pl.BlockSpec((1, HEAD_DIM, SEQ), lambda h, bi: (bi, h, 0)),
        out_shape=jax.ShapeDtypeStruct(qt.shape, bf16),
        scratch_shapes=[
            pltpu.VMEM((2 * NA_KH - 1, GRID_W, 128), f32),
            pltpu.VMEM((3, NA_BAND, NA_QBLK), f32),
            pltpu.VMEM((META_PAD + NA_BAND, NA_QBLK), f32),
        ],
        compiler_params=_params(("arbitrary", "arbitrary")),
        name="na_attn",
    )(rpb, qt, k_real, k_meta, vt_real, vt_meta)


def _na_meta_kernel(q_ref, km_ref, vtm_ref, o_ref):
    row = lax.broadcasted_iota(jnp.int32, (2 * HEAD_DIM, META_PAD), 0)
    meta_valid = row < N_META
    for h in range(NA_HEADS):
        lo = h * HEAD_DIM
        q = q_ref[0, lo:lo + HEAD_DIM, :]
        sel = (row >= HEAD_DIM) == (h % 2 == 1)
        qext = jnp.where(sel, jnp.concatenate([q, q], axis=0), jnp.zeros((), bf16))
        s = jnp.where(meta_valid, _nn(km_ref[0, h // 2], qext), NEG)
        m = jnp.max(s, axis=0, keepdims=True)
        p = jnp.exp2(s - m)
        l = jnp.sum(p, axis=0, keepdims=True)
        acc = _nn(vtm_ref[0, lo:lo + HEAD_DIM, :], p.astype(bf16))
        o_ref[0, lo:lo + HEAD_DIM, :] = (acc / l).astype(bf16)


def _na_meta_call(qt_meta, k_meta, vt_meta, b):
    return pl.pallas_call(
        _na_meta_kernel,
        grid=(b,),
        in_specs=[
            pl.BlockSpec((1, NA_WIDTH, META_PAD), lambda bi: (0, 0, bi)),
            pl.BlockSpec((1, NA_HEADS // 2, META_PAD, 128), lambda bi: (0, 0, bi, 0)),
            pl.BlockSpec((1, NA_WIDTH, META_PAD), lambda bi: (0, 0, bi)),
        ],
        out_specs=pl.BlockSpec((1, NA_WIDTH, META_PAD), lambda bi: (0, 0, bi)),
        out_shape=jax.ShapeDtypeStruct(qt_meta.shape, bf16),
        compiler_params=_params(("parallel",)),
        name="na_meta",
    )(qt_meta, k_meta, vt_meta)


def _out_kernel(x_ref, g_ref, ata_ref, atb_ref, w2_ref, woa_ref, wob_ref, wout_ref, fg_ref,
                o_ref, ht_ref, *, final):
    x = x_ref[0]
    ht_ref[...] = _rms_rows(x, g_ref[...]).T.astype(bf16)

    def gated(at_ref, z):
        return (at_ref[...].reshape(z.shape).astype(f32) * (z * _sigmoid(z))).astype(bf16)

    za = _nn(w2_ref[P2_ZA:P2_ZB, :], ht_ref[...])
    zb = _nn(w2_ref[P2_ZB:P2_GA, :], ht_ref[...])
    ga = _nn(w2_ref[P2_GA:P2_GB, :], ht_ref[...])
    gb = _nn(w2_ref[P2_GB:P2_END, :], ht_ref[...])
    ya = _nn(woa_ref[...], gated(ata_ref, za))
    yb = _nn(wob_ref[...], gated(atb_ref, zb))
    mix = (_sigmoid(ga) * ya + _sigmoid(gb) * yb).astype(bf16)
    out = x + _tn(mix, wout_ref[...])
    if final:
        out = _rms_rows(out, fg_ref[...])
    o_ref[0] = out


def _out_call(x3, norm_g, at_a, at_b, w2t, woat, wobt, wout, final_g, tm, final):
    bx, lx, _ = x3.shape
    const = lambda b, j: (0, 0)
    return pl.pallas_call(
        functools.partial(_out_kernel, final=final),
        grid=(bx, lx // tm),
        in_specs=[
            pl.BlockSpec((1, tm, D_MODEL), lambda b, j: (b, j, 0)),
            pl.BlockSpec((1, D_MODEL), const),
            pl.BlockSpec((1, 1, ATTN_WIDTH, tm), lambda b, j: (b, j, 0, 0)),
            pl.BlockSpec((1, NA_WIDTH, tm), lambda b, j: (b, 0, j)),
            pl.BlockSpec((P2_END, D_MODEL), const),
            pl.BlockSpec((D_MODEL, ATTN_WIDTH), const),
            pl.BlockSpec((D_MODEL, NA_WIDTH), const),
            pl.BlockSpec((D_MODEL, D_MODEL), const),
            pl.BlockSpec((1, D_MODEL), const),
        ],
        out_specs=pl.BlockSpec((1, tm, D_MODEL), lambda b, j: (b, j, 0)),
        out_shape=jax.ShapeDtypeStruct(x3.shape, f32),
        scratch_shapes=[pltpu.VMEM((D_MODEL, tm), bf16)],
        compiler_params=_params(("parallel", "parallel")),
        name="gate_out_proj",
    )(x3, norm_g, at_a, at_b, w2t, woat, wobt, wout, final_g)


def _rope_tables():
    t = np.arange(SEQ)
    axis_dim = HEAD_DIM // 2
    inv = jnp.asarray(ROPE_THETA, f32) ** (-jnp.arange(0, axis_dim, 2, dtype=f32) / axis_dim)
    ang_r = jnp.asarray(t // GRID_W, f32)[None, :] * inv[:, None]
    ang_c = jnp.asarray(t % GRID_W, f32)[None, :] * inv[:, None]
    ang = jnp.concatenate([ang_r, ang_r, ang_c, ang_c], axis=0)
    sign = np.repeat(np.array([-1.0, 1.0, -1.0, 1.0], np.float32), axis_dim // 2)[:, None]
    return jnp.cos(ang), jnp.sin(ang) * sign


def kernel(x, meta_tokens, norm_g, w_in, q_norm_g, k_norm_g, na_rpb, w_o_attn, w_o_na, w_out,
           final_norm_g):
    b = x.shape[0]
    depth = w_in.shape[0]
    cos_r, sin_r = _rope_tables()
    cos_m = jnp.ones((HEAD_DIM, b * META_PAD), f32)
    sin_m = jnp.zeros((HEAD_DIM, b * META_PAD), f32)

    meta_pad = jnp.concatenate(
        [meta_tokens.astype(f32), jnp.zeros((META_PAD - N_META, D_MODEL), f32)], axis=0)
    xm = jnp.tile(meta_pad, (b, 1))[None]
    xr = x
    fg = final_norm_g.reshape(1, D_MODEL)

    for i in range(depth):
        wi = w_in[i]
        qa, ka, va, za, qb, kb, vb, zb, ga, gb = (
            wi[:, 0:512], wi[:, 512:640], wi[:, 640:768], wi[:, 768:1280], wi[:, 1280:1792],
            wi[:, 1792:2304], wi[:, 2304:2816], wi[:, 2816:3328], wi[:, 3328:4352], wi[:, 4352:5376])
        w1t = jnp.concatenate([qa, ka, va, qb, kb, vb], axis=1).T.astype(bf16)
        w2t = jnp.concatenate([za, zb, ga, gb], axis=1).T.astype(bf16)
        woat = w_o_attn[i].T.astype(bf16)
        wobt = w_o_na[i].T.astype(bf16)
        wout = w_out[i].astype(bf16)
        ng = norm_g[i].reshape(1, D_MODEL)
        qg = q_norm_g[i].reshape(HEAD_DIM, 1)
        kg = k_norm_g[i].reshape(HEAD_DIM, 1)
        last = i == depth - 1

        qta, k_a, vta, qtb, k_b, vtb = _proj_call(xr, ng, w1t, qg, kg, cos_r, sin_r, TOKEN_TILE)
        qta_m, k_a_m, vta_m, qtb_m, k_b_m, vtb_m = _proj_call(
            xm, ng, w1t, qg, kg, cos_m, sin_m, TOKEN_TILE)

        at_a = _gqa_call(qta, k_a, k_a_m, vta, vta_m, meta_queries=False)
        at_b = _na_call(qtb, k_b, k_b_m, vtb, vtb_m, na_rpb[i])
        xr_new = _out_call(xr, ng, at_a, at_b, w2t, woat, wobt, wout, fg, TOKEN_TILE, last)
        if not last:
            at_a_m = _unpack_meta_outputs(
                _gqa_call(_pack_meta_queries(qta_m, b), k_a, k_a_m, vta, vta_m, meta_queries=True),
                qta_m)
            at_b_m = _na_meta_call(qtb_m, k_b_m, vtb_m, b)
            xm = _out_call(xm, ng, at_a_m, at_b_m, w2t, woat, wobt, wout, fg, TOKEN_TILE, False)
        xr = xr_new
    return xr
```
